```python
import jax, jax.numpy as jnp
from jax import lax
import numpy as np

D_MODEL = 1024
BATCH = 4
SEQ = 4096
DEPTH = 2

GLA_HEADS = 4
GLA_DK = 48
GLA_DV = 96
GLA_LOWRANK = 16
GLA_GATE_NORMALIZER = 16.0
SB_HEADS = 6
SB_DH = 64
SB_BLOCK = 128
HG_HEADS = 4
HG_DK = 128
HG_DV = 64
CHUNK = 16

MIX_WIDTH = GLA_HEADS * GLA_DV + SB_HEADS * SB_DH + HG_HEADS * HG_DV
D_FF = -(-(8 * D_MODEL) // (3 * 256)) * 256
RMS_EPS = 1e-6

IN_SIZES = (
    GLA_HEADS * GLA_DK,
    GLA_HEADS * GLA_DK,
    GLA_HEADS * GLA_DV,
    GLA_LOWRANK,
    GLA_HEADS * GLA_DV,
    SB_HEADS * SB_DH,
    SB_HEADS * SB_DH,
    SB_HEADS * SB_DH,
    HG_HEADS * HG_DK,
    HG_HEADS * HG_DK,
    HG_HEADS * HG_DV,
    HG_HEADS * HG_DV,
)
IN_COLS = int(sum(IN_SIZES))
SPLIT_IDX = tuple(int(c) for c in np.cumsum(IN_SIZES)[:-1])

kernel_name = "hybrid_gla_stickbreak_hgrn2_block"


def rms_norm(x, g):
    xf = x.astype(jnp.float32)
    y = xf * lax.rsqrt(jnp.mean(xf * xf, axis=-1, keepdims=True) + RMS_EPS)
    return (y * g.astype(jnp.float32)).astype(x.dtype)


def split_heads(t, n_heads):
    b, s, _ = t.shape
    return t.reshape(b, s, n_heads, -1).transpose(0, 2, 1, 3)


def merge_heads(t):
    b, h, s, d = t.shape
    return t.transpose(0, 2, 1, 3).reshape(b, s, h * d)


def chunked_gated_linear_attention(q, k, v, log_f):
    b, h, s, dk = q.shape
    dv = v.shape[-1]
    n = s // CHUNK
    q = q.reshape(b, h, n, CHUNK, dk)
    k = k.reshape(b, h, n, CHUNK, dk)
    v = v.reshape(b, h, n, CHUNK, dv)
    g_cum = jnp.cumsum(log_f.reshape(b, h, n, CHUNK, dk), axis=3)
    g_last = g_cum[:, :, :, -1:, :]
    q_dec = q * jnp.exp(g_cum)
    k_to_end = k * jnp.exp(g_last - g_cum)
    causal = jnp.tril(jnp.ones((CHUNK, CHUNK), dtype=bool))[:, :, None]
    rel = g_cum[:, :, :, :, None, :] - g_cum[:, :, :, None, :, :]
    rel_decay = jnp.exp(jnp.where(causal, rel, -jnp.inf))
    scores = jnp.einsum('bhnik,bhnjk,bhnijk->bhnij', q, k, rel_decay)
    o_intra = jnp.einsum('bhnij,bhnjv->bhniv', scores, v)
    chunk_states = jnp.einsum('bhnjk,bhnjv->bhnkv', k_to_end, v)
    chunk_decay = jnp.exp(g_last[:, :, :, 0, :])

    def step(state, inp):
        dec, upd = inp
        return dec[..., None] * state + upd, state

    init = jnp.zeros((b, h, dk, dv), jnp.float32)
    _, prev_states = lax.scan(step, init, (jnp.moveaxis(chunk_decay, 2, 0),
                                           jnp.moveaxis(chunk_states, 2, 0)))
    prev_states = jnp.moveaxis(prev_states, 0, 2)
    o_inter = jnp.einsum('bhnik,bhnkv->bhniv', q_dec, prev_states)
    return (o_intra + o_inter).reshape(b, h, s, dv)


def stick_breaking_attention(q, k, v):
    b, h, s, d = q.shape
    scale = d ** -0.5
    outs = []
    for start in range(0, s, SB_BLOCK):
        end = start + SB_BLOCK
        qb = q[:, :, start:end]
        kb = k[:, :, :end]
        vb = v[:, :, :end]
        z = jnp.einsum('bhqd,bhkd->bhqk', qb, kb) * scale
        t_pos = start + jnp.arange(SB_BLOCK)
        s_pos = jnp.arange(end)
        mask = s_pos[None, :] < t_pos[:, None]
        log_keep = jnp.where(mask, jax.nn.log_sigmoid(-z), 0.0)
        log_rest = lax.cumsum(log_keep, axis=3, reverse=True) - log_keep
        log_w = jnp.where(mask, jax.nn.log_sigmoid(z) + log_rest, -jnp.inf)
        weights = jnp.exp(log_w)
        outs.append(jnp.einsum('bhqk,bhkd->bhqd', weights, vb))
    return jnp.concatenate(outs, axis=2)


def setup_inputs(seed: int = 0) -> dict:
    key = jax.random.key(seed)
    ks = jax.random.split(key, 16)
    f32 = jnp.float32

    def gain(k_, shape):
        return 1.0 + 0.02 * jax.random.normal(k_, shape, f32)

    return {
        "x": jax.random.normal(ks[0], (BATCH, SEQ, D_MODEL), f32),
        "norm_mix_g": gain(ks[1], (DEPTH, D_MODEL)),
        "w_in": jax.random.normal(ks[2], (DEPTH, D_MODEL, IN_COLS), f32) * D_MODEL ** -0.5,
        "gla_w_decay": jax.random.normal(ks[3], (DEPTH, GLA_LOWRANK, GLA_HEADS * GLA_DK), f32) * GLA_LOWRANK ** -0.5,
        "gla_b_decay": 0.01 * jax.random.normal(ks[4], (DEPTH, GLA_HEADS * GLA_DK), f32),
        "gla_out_g": gain(ks[5], (DEPTH, GLA_DV)),
        "sb_q_g": gain(ks[6], (DEPTH, SB_DH)),
        "sb_k_g": gain(ks[7], (DEPTH, SB_DH)),
        "sb_out_g": gain(ks[8], (DEPTH, SB_DH)),
        "hg_out_g": gain(ks[9], (DEPTH, HG_DV)),
        "hg_lb_logits": 0.1 * jax.random.normal(ks[10], (DEPTH, HG_HEADS * HG_DK), f32),
        "w_out": jax.random.normal(ks[11], (DEPTH, MIX_WIDTH, D_MODEL), f32) * MIX_WIDTH ** -0.5,
        "norm_ffn_g": gain(ks[12], (DEPTH, D_MODEL)),
        "w_ffn_up": jax.random.normal(ks[13], (DEPTH, D_MODEL, 2 * D_FF), f32) * D_MODEL ** -0.5,
        "w_ffn_down": jax.random.normal(ks[14], (DEPTH, D_FF, D_MODEL), f32) * D_FF ** -0.5,
    }


def reference(x, norm_mix_g, w_in, gla_w_decay, gla_b_decay, gla_out_g, sb_q_g,
              sb_k_g, sb_out_g, hg_out_g, hg_lb_logits, w_out, norm_ffn_g,
              w_ffn_up, w_ffn_down):
    f32 = jnp.float32
    lb_probs = jax.nn.softmax(hg_lb_logits.astype(f32), axis=0)
    lower_bounds = jnp.cumsum(lb_probs, axis=0) - lb_probs[0:1]

    for li in range(DEPTH):
        h = rms_norm(x, norm_mix_g[li])
        proj = h @ w_in[li]
        (gq, gk, gv, g_lr, g_gate, sq, sk, sv,
         hq, hf, hi, h_gate) = jnp.split(proj.astype(f32), SPLIT_IDX, axis=-1)

        log_alpha = jax.nn.log_sigmoid(g_lr @ gla_w_decay[li].astype(f32)
                                       + gla_b_decay[li].astype(f32)) / GLA_GATE_NORMALIZER
        o_a = chunked_gated_linear_attention(
            split_heads(gq, GLA_HEADS) * GLA_DK ** -0.5,
            split_heads(gk, GLA_HEADS),
            split_heads(gv, GLA_HEADS),
            split_heads(log_alpha, GLA_HEADS))
        o_a = merge_heads(rms_norm(o_a, gla_out_g[li])) * jax.nn.silu(g_gate)

        q_b = rms_norm(split_heads(sq, SB_HEADS), sb_q_g[li])
        k_b = rms_norm(split_heads(sk, SB_HEADS), sb_k_g[li])
        o_b = stick_breaking_attention(q_b, k_b, split_heads(sv, SB_HEADS))
        o_b = merge_heads(rms_norm(o_b, sb_out_g[li]))

        lb = lower_bounds[li]
        log_sig = jax.nn.log_sigmoid(hf)
        log_lb = jnp.log(jnp.maximum(lb, 1e-30))
        log_f = jnp.where(lb > 0.0,
                          jnp.logaddexp(log_lb, jnp.log1p(-lb) + log_sig),
                          log_sig)
        k_c = -jnp.expm1(log_f)
        o_c = chunked_gated_linear_attention(
            split_heads(hq, HG_HEADS),
            split_heads(k_c, HG_HEADS),
            split_heads(hi, HG_HEADS),
            split_heads(log_f, HG_HEADS))
        o_c = merge_heads(rms_norm(o_c, hg_out_g[li])) * jax.nn.silu(h_gate)

        mixed = jnp.concatenate([o_a, o_b, o_c], axis=-1).astype(x.dtype)
        x = x + mixed @ w_out[li]

        hn = rms_norm(x, norm_ffn_g[li])
        gate, up = jnp.split(hn @ w_ffn_up[li], 2, axis=-1)
        x = x + (jax.nn.silu(gate) * up) @ w_ffn_down[li]
    return x
```

```python
import functools

import numpy as np
import jax
import jax.numpy as jnp
from jax import lax
from jax.experimental import pallas as pl
from jax.experimental.pallas import tpu as pltpu

F32 = jnp.float32
BF16 = jnp.bfloat16

LANES = 128
RMS_EPS = 1e-6

GLA_HEADS, GLA_DK, GLA_DV, GLA_LOWRANK = 4, 48, 96, 16
GLA_GATE_NORMALIZER = 16.0
SB_HEADS, SB_DH, SB_BLOCK = 6, 64, 128
HG_HEADS, HG_DK, HG_DV = 4, 128, 64

GLA_DK_PAD = 64
HEAD_V_PAD = LANES
MIX_CHUNK = 128
VMEM_LIMIT = 48 * 1024 * 1024

_SEGS = (("gv", 512), ("gg", 512), ("hq", 512), ("hf", 512), ("hi", 512), ("hg", 512),
         ("gq", 256), ("gk", 256), ("sq", 384), ("sk", 384), ("sv", 384), ("glr", 128))
_SEG_OFF = {}
_off = 0
for _name, _w in _SEGS:
    _SEG_OFF[_name] = (_off, _w)
    _off += _w
PROJ_COLS = _off


def _padded_head_cols(start, heads, width, pad):
    idx = -np.ones((heads, pad), np.int64)
    idx[:, :width] = start + np.arange(heads)[:, None] * width + np.arange(width)[None, :]
    return idx.reshape(-1)


def _proj_source_columns():
    sizes = (GLA_HEADS * GLA_DK, GLA_HEADS * GLA_DK, GLA_HEADS * GLA_DV, GLA_LOWRANK,
             GLA_HEADS * GLA_DV, SB_HEADS * SB_DH, SB_HEADS * SB_DH, SB_HEADS * SB_DH,
             HG_HEADS * HG_DK, HG_HEADS * HG_DK, HG_HEADS * HG_DV, HG_HEADS * HG_DV)
    starts = np.concatenate([[0], np.cumsum(sizes)[:-1]])
    (gq, gk, gv, glr, gg, sq, sk, sv, hq, hf, hi, hg) = [int(s) for s in starts]
    src = {
        "gq": _padded_head_cols(gq, GLA_HEADS, GLA_DK, GLA_DK_PAD),
        "gk": _padded_head_cols(gk, GLA_HEADS, GLA_DK, GLA_DK_PAD),
        "gv": _padded_head_cols(gv, GLA_HEADS, GLA_DV, HEAD_V_PAD),
        "gg": _padded_head_cols(gg, GLA_HEADS, GLA_DV, HEAD_V_PAD),
        "glr": _padded_head_cols(glr, 1, GLA_LOWRANK, LANES),
        "sq": np.arange(sq, sq + SB_HEADS * SB_DH),
        "sk": np.arange(sk, sk + SB_HEADS * SB_DH),
        "sv": np.arange(sv, sv + SB_HEADS * SB_DH),
        "hq": np.arange(hq, hq + HG_HEADS * HG_DK),
        "hf": np.arange(hf, hf + HG_HEADS * HG_DK),
        "hi": _padded_head_cols(hi, HG_HEADS, HG_DV, HEAD_V_PAD),
        "hg": _padded_head_cols(hg, HG_HEADS, HG_DV, HEAD_V_PAD),
    }
    return np.concatenate([src[name] for name, _ in _SEGS])


_PROJ_SRC = _proj_source_columns()


def _take_padded(arr, src, axis):
    taken = jnp.take(arr, jnp.asarray(np.maximum(src, 0)), axis=axis)
    shape = [1] * arr.ndim
    shape[axis] = len(src)
    return jnp.where(jnp.asarray(src >= 0).reshape(shape), taken, 0)


def _dot(a, b):
    return jnp.dot(a, b, preferred_element_type=F32)


def _dot_nt(a, b):
    return lax.dot_general(a, b, (((1,), (1,)), ((), ())), preferred_element_type=F32)


def _split_bf16(x):
    hi = x.astype(BF16)
    lo = (x - hi.astype(F32)).astype(BF16)
    return hi, lo


def _dot_split_rhs(m, x):
    hi, lo = _split_bf16(x)
    return _dot(m, hi) + _dot(m, lo)


def _dot_split_lhs(x, m):
    hi, lo = _split_bf16(x)
    return _dot(hi, m) + _dot(lo, m)


def _softplus(z):
    return jnp.maximum(z, 0.0) + jnp.log(1.0 + jnp.exp(-jnp.abs(z)))


def _silu(z):
    return z / (1.0 + jnp.exp(-z))


def _rms_rows(x, g):
    ms = jnp.mean(x * x, axis=-1, keepdims=True)
    return x * lax.rsqrt(ms + RMS_EPS) * g


def _norm_matmul_kernel(x_ref, g_ref, w_ref, o_ref, h_scr):
    @pl.when(pl.program_id(1) == 0)
    def _():
        h_scr[...] = _rms_rows(x_ref[...], g_ref[...]).astype(BF16)

    o_ref[...] = _dot(h_scr[...], w_ref[...]).astype(o_ref.dtype)


def _norm_matmul(x, g, w, *, tm, tn, out_dtype):
    m, d = x.shape
    n = w.shape[1]
    return pl.pallas_call(
        _norm_matmul_kernel,
        grid=(m // tm, n // tn),
        in_specs=[pl.BlockSpec((tm, d), lambda i, j: (i, 0)),
                  pl.BlockSpec((1, d), lambda i, j: (0, 0)),
                  pl.BlockSpec((d, tn), lambda i, j: (0, j))],
        out_specs=pl.BlockSpec((tm, tn), lambda i, j: (i, j)),
        out_shape=jax.ShapeDtypeStruct((m, n), out_dtype),
        scratch_shapes=[pltpu.VMEM((tm, d), BF16)],
        compiler_params=pltpu.CompilerParams(
            dimension_semantics=("arbitrary", "arbitrary"), vmem_limit_bytes=VMEM_LIMIT),
        name="norm_matmul",
    )(x, g.reshape(1, d), w)


def _ffn_up_kernel(x_ref, g_ref, wg_ref, wu_ref, o_ref, h_scr):
    @pl.when(pl.program_id(1) == 0)
    def _():
        h_scr[...] = _rms_rows(x_ref[...], g_ref[...]).astype(BF16)

    h = h_scr[...]
    gate = _dot(h, wg_ref[...])
    up = _dot(h, wu_ref[...])
    o_ref[...] = (_silu(gate) * up).astype(o_ref.dtype)


def _ffn_up(x, g, w_up, *, tm, tn):
    m, d = x.shape
    d_ff = w_up.shape[1] // 2
    nj = d_ff // tn
    return pl.pallas_call(
        _ffn_up_kernel,
        grid=(m // tm, nj),
        in_specs=[pl.BlockSpec((tm, d), lambda i, j: (i, 0)),
                  pl.BlockSpec((1, d), lambda i, j: (0, 0)),
                  pl.BlockSpec((d, tn), lambda i, j: (0, j)),
                  pl.BlockSpec((d, tn), lambda i, j: (0, j + nj))],
        out_specs=pl.BlockSpec((tm, tn), lambda i, j: (i, j)),
        out_shape=jax.ShapeDtypeStruct((m, d_ff), BF16),
        scratch_shapes=[pltpu.VMEM((tm, d), BF16)],
        compiler_params=pltpu.CompilerParams(
            dimension_semantics=("arbitrary", "arbitrary"), vmem_limit_bytes=VMEM_LIMIT),
        name="ffn_up",
    )(x, g.reshape(1, d), w_up, w_up)


def _residual_matmul_kernel(n_in, res_ref, *refs):
    a_refs, w_refs, o_ref = refs[:n_in], refs[n_in:2 * n_in], refs[2 * n_in]
    acc = res_ref[...]
    for a_ref, w_ref in zip(a_refs, w_refs):
        acc = acc + _dot(a_ref[...], w_ref[...])
    o_ref[...] = acc


def _residual_matmul(res, acts, weights, *, tm):
    m, n = res.shape
    n_in = len(acts)
    in_specs = [pl.BlockSpec((tm, n), lambda i: (i, 0))]
    in_specs += [pl.BlockSpec((tm, a.shape[1]), lambda i: (i, 0)) for a in acts]
    in_specs += [pl.BlockSpec(w.shape, lambda i: (0, 0)) for w in weights]
    return pl.pallas_call(
        functools.partial(_residual_matmul_kernel, n_in),
        grid=(m // tm,),
        in_specs=in_specs,
        out_specs=pl.BlockSpec((tm, n), lambda i: (i, 0)),
        out_shape=jax.ShapeDtypeStruct((m, n), F32),
        compiler_params=pltpu.CompilerParams(
            dimension_semantics=("arbitrary",), vmem_limit_bytes=VMEM_LIMIT),
        name="residual_matmul",
    )(res, *acts, *weights)


def _decay_sum_matrix(chunk):
    t = np.arange(chunk)
    blocks = [(t[None, :] <= t[:, None]), (t[None, :] > t[:, None])]
    h = 1
    while h < chunk:
        mid = (t // (2 * h)) * (2 * h) + h
        right = t >= mid
        m = np.where(right[:, None],
                     (t[None, :] >= mid[:, None]) & (t[None, :] <= t[:, None]),
                     (t[None, :] > t[:, None]) & (t[None, :] < mid[:, None]))
        blocks.append(m)
        h *= 2
    return np.concatenate(blocks, axis=0).astype(np.float32)


def _gated_linear_chunk(q, k, v, lf, msum_ref, state_ref, *, heads, dk_lanes):
    c = q.shape[0]
    n_levels = c.bit_length() - 1
    sums = _dot_split_rhs(msum_ref[...], lf)
    decays = jnp.exp(sums)
    g_last = sums[c - 1:c, :]
    chunk_decay = jnp.exp(g_last)
    q_in = (q * decays[0:c]).astype(BF16)
    k_end = (k * decays[c:2 * c]).astype(BF16)
    row = lax.broadcasted_iota(jnp.int32, (c, 1), 0)
    q_lv, k_lv = [q.astype(BF16)], [k.astype(BF16)]
    for lv in range(n_levels):
        h = 1 << lv
        e = decays[(2 + lv) * c:(3 + lv) * c]
        right = (row & h) != 0
        q_lv.append(jnp.where(right, q * e, 0.0).astype(BF16))
        k_lv.append(jnp.where(right, 0.0, k * e).astype(BF16))

    ri = lax.broadcasted_iota(jnp.int32, (c, c), 0)
    ci = lax.broadcasted_iota(jnp.int32, (c, c), 1)
    pair_masks = [ri == ci] + [(ri >> (lv + 1)) == (ci >> (lv + 1)) for lv in range(n_levels)]
    lane = lax.broadcasted_iota(jnp.int32, (1, LANES), 1)

    outs = []
    for hd in range(heads):
        g0 = (hd * dk_lanes // LANES) * LANES
        grp = slice(g0, g0 + LANES)
        if dk_lanes < LANES:
            lo = hd * dk_lanes - g0
            head_lanes = (lane >= lo) & (lane < lo + dk_lanes)
            pick = lambda a: jnp.where(head_lanes, a[:, grp], jnp.zeros_like(a[:, grp]))
        else:
            pick = lambda a: a[:, grp]
        scores = jnp.zeros((c, c), F32)
        for ql, kl, msk in zip(q_lv, k_lv, pair_masks):
            scores = scores + jnp.where(msk, _dot_nt(pick(ql), kl[:, grp]), 0.0)
        v_h = v[:, hd * LANES:(hd + 1) * LANES].astype(BF16)
        state = state_ref[hd]
        o = _dot(scores.astype(BF16), v_h) + _dot_nt(pick(q_in), state.astype(BF16))
        upd = lax.dot_general(v_h, k_end[:, grp], (((0,), (0,)), ((), ())),
                              preferred_element_type=F32)
        state_ref[hd] = state * chunk_decay[:, grp] + upd
        outs.append(o)
    return outs


def _finish_heads(outs, gate, gain, dv, o_ref):
    for hd, o in enumerate(outs):
        sl = slice(hd * LANES, (hd + 1) * LANES)
        ms = jnp.sum(o * o, axis=-1, keepdims=True) * (1.0 / dv)
        y = o * lax.rsqrt(ms + RMS_EPS) * gain
        o_ref[:, sl] = (y * _silu(gate[:, sl])).astype(o_ref.dtype)


def _gla_kernel(q_ref, k_ref, v_ref, gate_ref, lr_ref, wd_ref, bd_ref, gain_ref, msum_ref,
                o_ref, state_ref):
    @pl.when(pl.program_id(1) == 0)
    def _():
        state_ref[...] = jnp.zeros_like(state_ref)

    lr = lr_ref[...]
    lr_hi, lr_lo = _split_bf16(lr)
    wd_hi, wd_lo = _split_bf16(wd_ref[...])
    logits = _dot(lr_hi, wd_hi) + _dot(lr_hi, wd_lo) + _dot(lr_lo, wd_hi) + bd_ref[...]
    lf = -_softplus(-logits) * (1.0 / GLA_GATE_NORMALIZER)
    outs = _gated_linear_chunk(q_ref[...] * GLA_DK ** -0.5, k_ref[...], v_ref[...], lf,
                               msum_ref, state_ref, heads=GLA_HEADS, dk_lanes=GLA_DK_PAD)
    _finish_heads(outs, gate_ref[...], gain_ref[...], GLA_DV, o_ref)


def _hg_kernel(layer, q_ref, f_ref, v_ref, gate_ref, lb_ref, gain_ref, msum_ref,
               o_ref, state_ref):
    @pl.when(pl.program_id(1) == 0)
    def _():
        state_ref[...] = jnp.zeros_like(state_ref)

    logits = lb_ref[...]
    ex = jnp.exp(logits - jnp.max(logits, axis=0, keepdims=True))
    probs = ex / jnp.sum(ex, axis=0, keepdims=True)
    lb = jnp.zeros_like(probs[0:1])
    for d in range(1, layer + 1):
        lb = lb + probs[d:d + 1]

    hf = f_ref[...]
    sp = _softplus(-hf)
    log_sig = -sp
    a = jnp.log(jnp.maximum(lb, 1e-30))
    b = jnp.log(1.0 - lb) + log_sig
    lae = jnp.maximum(a, b) + jnp.log(1.0 + jnp.exp(-jnp.abs(a - b)))
    lf = jnp.where(lb > 0.0, lae, log_sig)
    k = (1.0 - lb) * jnp.exp(-(hf + sp))
    outs = _gated_linear_chunk(q_ref[...], k, v_ref[...], lf, msum_ref, state_ref,
                               heads=HG_HEADS, dk_lanes=HG_DK)
    _finish_heads(outs, gate_ref[...], gain_ref[...], HG_DV, o_ref)


def _proj_spec(name, rows, n_chunks):
    off, width = _SEG_OFF[name]
    assert off % width == 0
    return pl.BlockSpec((rows, width), lambda b, c: (b * n_chunks + c, off // width))


def _mixer_call(kernel_fn, proj, seg_names, extras, *, batch, heads, name):
    m = proj.shape[0]
    n_chunks = m // batch // MIX_CHUNK
    msum = jnp.asarray(_decay_sum_matrix(MIX_CHUNK), BF16)
    in_specs = [_proj_spec(s, MIX_CHUNK, n_chunks) for s in seg_names]
    in_specs += [pl.BlockSpec(e.shape, lambda b, c: (0, 0)) for e in extras]
    in_specs += [pl.BlockSpec(msum.shape, lambda b, c: (0, 0))]
    width = heads * HEAD_V_PAD
    return pl.pallas_call(
        kernel_fn,
        grid=(batch, n_chunks),
        in_specs=in_specs,
        out_specs=pl.BlockSpec((MIX_CHUNK, width), lambda b, c: (b * n_chunks + c, 0)),
        out_shape=jax.ShapeDtypeStruct((m, width), BF16),
        scratch_shapes=[pltpu.VMEM((heads, HEAD_V_PAD, LANES), F32)],
        compiler_params=pltpu.CompilerParams(
            dimension_semantics=("arbitrary", "arbitrary"), vmem_limit_bytes=VMEM_LIMIT),
        name=name,
    )(*([proj] * len(seg_names)), *extras, msum)


def _pair_rms(x, g, lo_lanes):
    x2 = x * x
    s_lo = jnp.sum(jnp.where(lo_lanes, x2, 0.0), axis=-1, keepdims=True)
    s_hi = jnp.sum(jnp.where(lo_lanes, 0.0, x2), axis=-1, keepdims=True)
    ms = jnp.where(lo_lanes, s_lo, s_hi) * (1.0 / SB_DH)
    return x * lax.rsqrt(ms + RMS_EPS) * g


def _sb_kernel(q_ref, k_ref, v_ref, qg_ref, kg_ref, og_ref, u_ref, o_ref,
               kn_scr, vm_scr, c_scr, acc_scr):
    qi = pl.program_id(2)
    bq = q_ref.shape[0]
    n_kb = k_ref.shape[0] // bq
    lo_lanes = lax.broadcasted_iota(jnp.int32, (1, LANES), 1) < SB_DH

    @pl.when(qi == 0)
    def _():
        kn_scr[...] = _pair_rms(k_ref[...], kg_ref[...], lo_lanes).astype(BF16)
        v = v_ref[...].reshape(n_kb, bq, LANES)
        vm_scr[:, 0:bq, :] = jnp.where(lo_lanes, v, 0.0).astype(BF16)
        vm_scr[:, bq:2 * bq, :] = jnp.where(lo_lanes, 0.0, v).astype(BF16)

    qn = _pair_rms(q_ref[...], qg_ref[...], lo_lanes) * SB_DH ** -0.5
    q2 = jnp.concatenate([jnp.where(lo_lanes, qn, 0.0), jnp.where(lo_lanes, 0.0, qn)],
                         axis=0).astype(BF16)
    u2 = u_ref[...]

    def tile(kb, diagonal):
        kt = kn_scr[pl.ds(pl.multiple_of(kb * bq, bq), bq), :]
        z = _dot_nt(q2, kt)
        sp = _softplus(z)
        if diagonal:
            ri = lax.broadcasted_iota(jnp.int32, (2 * bq, bq), 0) & (bq - 1)
            ci = lax.broadcasted_iota(jnp.int32, (2 * bq, bq), 1)
            causal = ci < ri
            sp = jnp.where(causal, sp, 0.0)
        r2 = _dot_split_lhs(sp, u2)
        c_prev = c_scr[...]
        w = jnp.exp(z - sp - r2[:, :bq] - c_prev)
        if diagonal:
            w = jnp.where(causal, w, 0.0)
        c_scr[...] = c_prev + r2[:, bq:]
        wcat = jnp.concatenate([w[:bq], w[bq:]], axis=1).astype(BF16)
        acc_scr[...] += _dot(wcat, vm_scr[kb])

    c_scr[...] = jnp.zeros_like(c_scr)
    acc_scr[...] = jnp.zeros_like(acc_scr)
    tile(qi, True)

    def body(i, carry):
        tile(qi - 1 - i, False)
        return carry

    lax.fori_loop(0, qi, body, 0)
    o_ref[...] = _pair_rms(acc_scr[...], og_ref[...], lo_lanes).astype(o_ref.dtype)


def _sb_attention(proj, qg, kg, og, *, batch):
    m = proj.shape[0]
    t = m // batch
    bq = SB_BLOCK
    nq = t // bq
    n_pairs = SB_HEADS * SB_DH // LANES
    oq, ok, ov = (_SEG_OFF[s][0] // LANES for s in ("sq", "sk", "sv"))
    j = np.arange(bq)
    u2 = np.concatenate([(j[:, None] > j[None, :]), np.ones((bq, bq), bool)], axis=1)
    u2 = jnp.asarray(u2.astype(np.float32), BF16)
    pair = lambda g: jnp.tile(g, 2).reshape(1, LANES)
    const = lambda shape: pl.BlockSpec(shape, lambda b, p, i: (0, 0))
    return pl.pallas_call(
        _sb_kernel,
        grid=(batch, n_pairs, nq),
        in_specs=[pl.BlockSpec((bq, LANES), lambda b, p, i: (b * nq + i, oq + p)),
                  pl.BlockSpec((t, LANES), lambda b, p, i: (b, ok + p)),
                  pl.BlockSpec((t, LANES), lambda b, p, i: (b, ov + p)),
                  const((1, LANES)), const((1, LANES)), const((1, LANES)),
                  const((bq, 2 * bq))],
        out_specs=pl.BlockSpec((bq, LANES), lambda b, p, i: (b * nq + i, p)),
        out_shape=jax.ShapeDtypeStruct((m, n_pairs * LANES), BF16),
        scratch_shapes=[pltpu.VMEM((t, LANES), BF16),
                        pltpu.VMEM((nq, 2 * bq, LANES), BF16),
                        pltpu.VMEM((2 * bq, bq), F32),
                        pltpu.VMEM((bq, LANES), F32)],
        compiler_params=pltpu.CompilerParams(
            dimension_semantics=("arbitrary", "arbitrary", "arbitrary"),
            vmem_limit_bytes=VMEM_LIMIT),
        name="sb_attention",
    )(proj, proj, proj, pair(qg), pair(kg), pair(og), u2)


def kernel(x, norm_mix_g, w_in, gla_w_decay, gla_b_decay, gla_out_g, sb_q_g, sb_k_g, sb_out_g,
           hg_out_g, hg_lb_logits, w_out, norm_ffn_g, w_ffn_up, w_ffn_down):
    batch, seq, d_model = x.shape
    depth = w_in.shape[0]
    x2 = x.reshape(batch * seq, d_model).astype(F32)

    gla_cols = _padded_head_cols(0, GLA_HEADS, GLA_DK, GLA_DK_PAD)
    rows_a = _padded_head_cols(0, GLA_HEADS, GLA_DV, HEAD_V_PAD)
    sb_lo = GLA_HEADS * GLA_DV
    hg_lo = sb_lo + SB_HEADS * SB_DH
    rows_c = _padded_head_cols(hg_lo, HG_HEADS, HG_DV, HEAD_V_PAD)

    for li in range(depth):
        w_in_p = _take_padded(w_in[li], _PROJ_SRC, 1).astype(BF16)
        proj = _norm_matmul(x2, norm_mix_g[li], w_in_p, tm=512, tn=PROJ_COLS // 2, out_dtype=F32)

        wd = _take_padded(gla_w_decay[li].astype(F32), gla_cols, 1)
        wd = jnp.pad(wd, ((0, LANES - GLA_LOWRANK), (0, 0)))
        bd = _take_padded(gla_b_decay[li].astype(F32), gla_cols, 0).reshape(1, -1)
        gain_a = jnp.pad(gla_out_g[li].astype(F32), (0, HEAD_V_PAD - GLA_DV)).reshape(1, -1)
        o_a = _mixer_call(_gla_kernel, proj, ("gq", "gk", "gv", "gg", "glr"), (wd, bd, gain_a),
                          batch=batch, heads=GLA_HEADS, name="gla_mixer")

        o_b = _sb_attention(proj, sb_q_g[li].astype(F32), sb_k_g[li].astype(F32),
                            sb_out_g[li].astype(F32), batch=batch)

        gain_c = jnp.pad(hg_out_g[li].astype(F32), (0, HEAD_V_PAD - HG_DV)).reshape(1, -1)
        o_c = _mixer_call(functools.partial(_hg_kernel, li), proj, ("hq", "hf", "hi", "hg"),
                          (hg_lb_logits.astype(F32), gain_c),
                          batch=batch, heads=HG_HEADS, name="hg_mixer")

        wo = w_out[li]
        w_a = _take_padded(wo, rows_a, 0).astype(BF16)
        w_b = wo[sb_lo:hg_lo].astype(BF16)
        w_c = _take_padded(wo, rows_c, 0).astype(BF16)
        x2 = _residual_matmul(x2, (o_a, o_b, o_c), (w_a, w_b, w_c), tm=1024)

        act = _ffn_up(x2, norm_ffn_g[li], w_ffn_up[li].astype(BF16), tm=512, tn=1408)
        x2 = _residual_matmul(x2, (act,), (w_ffn_down[li].astype(BF16),), tm=512)
    return x2.reshape(batch, seq, d_model).astype(x.dtype)
```

```python
import functools

import numpy as np
import jax
import jax.numpy as jnp
from jax import lax
from jax.experimental import pallas as pl
from jax.experimental.pallas import tpu as pltpu

F32 = jnp.float32
BF16 = jnp.bfloat16

LANES = 128
RMS_EPS = 1e-6

GLA_HEADS, GLA_DK, GLA_DV, GLA_LOWRANK = 4, 48, 96, 16
GLA_GATE_NORMALIZER = 16.0
SB_HEADS, SB_DH, SB_BLOCK = 6, 64, 128
HG_HEADS, HG_DK, HG_DV = 4, 128, 64

GLA_DK_PAD = 64
HEAD_V_PAD = LANES
MIX_CHUNK = 128
SB_QGROUP = 4
SB_UNROLL = 2
LOG2_E = 1.4426950408889634
MASK_BIAS = -1e30
assert SB_QGROUP % SB_UNROLL == 0 and SB_QGROUP // SB_UNROLL >= 2
VMEM_LIMIT = 48 * 1024 * 1024

_SEGS = (("gv", 512), ("gg", 512), ("hq", 512), ("hf", 512), ("hi", 512), ("hg", 512),
         ("gq", 256), ("gk", 256), ("sq", 384), ("sk", 384), ("sv", 384), ("glr", 128))
_SEG_OFF = {}
_off = 0
for _name, _w in _SEGS:
    _SEG_OFF[_name] = (_off, _w)
    _off += _w
PROJ_COLS = _off


def _padded_head_cols(start, heads, width, pad):
    idx = -np.ones((heads, pad), np.int64)
    idx[:, :width] = start + np.arange(heads)[:, None] * width + np.arange(width)[None, :]
    return idx.reshape(-1)


def _proj_source_columns():
    sizes = (GLA_HEADS * GLA_DK, GLA_HEADS * GLA_DK, GLA_HEADS * GLA_DV, GLA_LOWRANK,
             GLA_HEADS * GLA_DV, SB_HEADS * SB_DH, SB_HEADS * SB_DH, SB_HEADS * SB_DH,
             HG_HEADS * HG_DK, HG_HEADS * HG_DK, HG_HEADS * HG_DV, HG_HEADS * HG_DV)
    starts = np.concatenate([[0], np.cumsum(sizes)[:-1]])
    (gq, gk, gv, glr, gg, sq, sk, sv, hq, hf, hi, hg) = [int(s) for s in starts]
    src = {
        "gq": _padded_head_cols(gq, GLA_HEADS, GLA_DK, GLA_DK_PAD),
        "gk": _padded_head_cols(gk, GLA_HEADS, GLA_DK, GLA_DK_PAD),
        "gv": _padded_head_cols(gv, GLA_HEADS, GLA_DV, HEAD_V_PAD),
        "gg": _padded_head_cols(gg, GLA_HEADS, GLA_DV, HEAD_V_PAD),
        "glr": _padded_head_cols(glr, 1, GLA_LOWRANK, LANES),
        "sq": np.arange(sq, sq + SB_HEADS * SB_DH),
        "sk": np.arange(sk, sk + SB_HEADS * SB_DH),
        "sv": np.arange(sv, sv + SB_HEADS * SB_DH),
        "hq": np.arange(hq, hq + HG_HEADS * HG_DK),
        "hf": np.arange(hf, hf + HG_HEADS * HG_DK),
        "hi": _padded_head_cols(hi, HG_HEADS, HG_DV, HEAD_V_PAD),
        "hg": _padded_head_cols(hg, HG_HEADS, HG_DV, HEAD_V_PAD),
    }
    return np.concatenate([src[name] for name, _ in _SEGS])


_PROJ_SRC = _proj_source_columns()


def _take_padded(arr, src, axis):
    taken = jnp.take(arr, jnp.asarray(np.maximum(src, 0)), axis=axis)
    shape = [1] * arr.ndim
    shape[axis] = len(src)
    return jnp.where(jnp.asarray(src >= 0).reshape(shape), taken, 0)


def _dot(a, b):
    return jnp.dot(a, b, preferred_element_type=F32)


def _dot_nt(a, b):
    return lax.dot_general(a, b, (((1,), (1,)), ((), ())), preferred_element_type=F32)


def _split_bf16(x):
    hi = x.astype(BF16)
    lo = (x - hi.astype(F32)).astype(BF16)
    return hi, lo


def _dot_split_rhs(m, x):
    hi, lo = _split_bf16(x)
    return _dot(m, hi) + _dot(m, lo)


def _dot_split_lhs(x, m):
    hi, lo = _split_bf16(x)
    return _dot(hi, m) + _dot(lo, m)


def _softplus(z):
    return jnp.maximum(z, 0.0) + jnp.log(1.0 + jnp.exp(-jnp.abs(z)))


def _silu(z):
    return z / (1.0 + jnp.exp(-z))


def _rms_rows(x, g):
    ms = jnp.mean(x * x, axis=-1, keepdims=True)
    return x * lax.rsqrt(ms + RMS_EPS) * g


def _norm_matmul_kernel(x_ref, g_ref, w_ref, o_ref, h_scr):
    @pl.when(pl.program_id(1) == 0)
    def _():
        h_scr[...] = _rms_rows(x_ref[...], g_ref[...]).astype(BF16)

    o_ref[...] = _dot(h_scr[...], w_ref[...]).astype(o_ref.dtype)


def _norm_matmul(x, g, w, *, tm, tn, out_dtype):
    m, d = x.shape
    n = w.shape[1]
    return pl.pallas_call(
        _norm_matmul_kernel,
        grid=(m // tm, n // tn),
        in_specs=[pl.BlockSpec((tm, d), lambda i, j: (i, 0)),
                  pl.BlockSpec((1, d), lambda i, j: (0, 0)),
                  pl.BlockSpec((d, tn), lambda i, j: (0, j))],
        out_specs=pl.BlockSpec((tm, tn), lambda i, j: (i, j)),
        out_shape=jax.ShapeDtypeStruct((m, n), out_dtype),
        scratch_shapes=[pltpu.VMEM((tm, d), BF16)],
        compiler_params=pltpu.CompilerParams(
            dimension_semantics=("arbitrary", "arbitrary"), vmem_limit_bytes=VMEM_LIMIT),
        name="norm_matmul",
    )(x, g.reshape(1, d), w)


def _ffn_up_kernel(x_ref, g_ref, wg_ref, wu_ref, o_ref, h_scr):
    @pl.when(pl.program_id(1) == 0)
    def _():
        h_scr[...] = _rms_rows(x_ref[...], g_ref[...]).astype(BF16)

    h = h_scr[...]
    gate = _dot(h, wg_ref[...])
    up = _dot(h, wu_ref[...])
    o_ref[...] = (_silu(gate) * up).astype(o_ref.dtype)


def _ffn_up(x, g, w_up, *, tm, tn):
    m, d = x.shape
    d_ff = w_up.shape[1] // 2
    nj = d_ff // tn
    return pl.pallas_call(
        _ffn_up_kernel,
        grid=(m // tm, nj),
        in_specs=[pl.BlockSpec((tm, d), lambda i, j: (i, 0)),
                  pl.BlockSpec((1, d), lambda i, j: (0, 0)),
                  pl.BlockSpec((d, tn), lambda i, j: (0, j)),
                  pl.BlockSpec((d, tn), lambda i, j: (0, j + nj))],
        out_specs=pl.BlockSpec((tm, tn), lambda i, j: (i, j)),
        out_shape=jax.ShapeDtypeStruct((m, d_ff), BF16),
        scratch_shapes=[pltpu.VMEM((tm, d), BF16)],
        compiler_params=pltpu.CompilerParams(
            dimension_semantics=("arbitrary", "arbitrary"), vmem_limit_bytes=VMEM_LIMIT),
        name="ffn_up",
    )(x, g.reshape(1, d), w_up, w_up)


def _residual_matmul_kernel(n_in, res_ref, *refs):
    a_refs, w_refs, o_ref = refs[:n_in], refs[n_in:2 * n_in], refs[2 * n_in]
    acc = res_ref[...]
    for a_ref, w_ref in zip(a_refs, w_refs):
        acc = acc + _dot(a_ref[...], w_ref[...])
    o_ref[...] = acc


def _residual_matmul(res, acts, weights, *, tm):
    m, n = res.shape
    n_in = len(acts)
    in_specs = [pl.BlockSpec((tm, n), lambda i: (i, 0))]
    in_specs += [pl.BlockSpec((tm, a.shape[1]), lambda i: (i, 0)) for a in acts]
    in_specs += [pl.BlockSpec(w.shape, lambda i: (0, 0)) for w in weights]
    return pl.pallas_call(
        functools.partial(_residual_matmul_kernel, n_in),
        grid=(m // tm,),
        in_specs=in_specs,
        out_specs=pl.BlockSpec((tm, n), lambda i: (i, 0)),
        out_shape=jax.ShapeDtypeStruct((m, n), F32),
        compiler_params=pltpu.CompilerParams(
            dimension_semantics=("arbitrary",), vmem_limit_bytes=VMEM_LIMIT),
        name="residual_matmul",
    )(res, *acts, *weights)


def _decay_sum_matrix(chunk):
    t = np.arange(chunk)
    blocks = [(t[None, :] <= t[:, None]), (t[None, :] > t[:, None])]
    h = 1
    while h < chunk:
        mid = (t // (2 * h)) * (2 * h) + h
        right = t >= mid
        m = np.where(right[:, None],
                     (t[None, :] >= mid[:, None]) & (t[None, :] <= t[:, None]),
                     (t[None, :] > t[:, None]) & (t[None, :] < mid[:, None]))
        blocks.append(m)
        h *= 2
    return np.concatenate(blocks, axis=0).astype(np.float32)


def _pair_owner_matrix(chunk):
    t = np.arange(chunk)
    x = t[:, None] ^ t[None, :]
    level = np.floor(np.log2(np.maximum(x, 1))).astype(np.int32)
    return np.where(x == 0, 0, np.where(t[None, :] < t[:, None], 1 + level, -1)).astype(np.int32)


def _gated_linear_chunk(q, k, v, lf, msum_ref, owner_ref, state_ref, *, heads, dk_lanes):
    c = q.shape[0]
    n_levels = c.bit_length() - 1
    sums = _dot(msum_ref[...], lf.astype(BF16))
    chunk_decay = jnp.exp(sums[c - 1:c, :])
    decays = jnp.exp(sums.astype(BF16))
    qb, kb = q.astype(BF16), k.astype(BF16)
    q_in = qb * decays[0:c]
    k_end = kb * decays[c:2 * c]
    q_lv = [qb] + [qb * decays[(2 + lv) * c:(3 + lv) * c] for lv in range(n_levels)]
    k_lv = [kb] + [kb * decays[(2 + lv) * c:(3 + lv) * c] for lv in range(n_levels)]
    owner = owner_ref[...]
    owned = [owner == lv for lv in range(n_levels + 1)]
    lane = lax.broadcasted_iota(jnp.int32, (1, LANES), 1)

    outs = []
    for hd in range(heads):
        g0 = (hd * dk_lanes // LANES) * LANES
        grp = slice(g0, g0 + LANES)
        if dk_lanes < LANES:
            lo = hd * dk_lanes - g0
            head_lanes = jnp.where((lane >= lo) & (lane < lo + dk_lanes), 1.0, 0.0).astype(BF16)
            pick = lambda a: a[:, grp] * head_lanes
        else:
            pick = lambda a: a[:, grp]
        scores = jnp.zeros((c, c), F32)
        for ql, kl, msk in zip(q_lv, k_lv, owned):
            scores = jnp.where(msk, _dot_nt(pick(ql), kl[:, grp]), scores)
        v_h = v[:, hd * LANES:(hd + 1) * LANES].astype(BF16)
        state = state_ref[hd]
        o = _dot(scores.astype(BF16), v_h) + _dot_nt(pick(q_in), state.astype(BF16))
        upd = lax.dot_general(v_h, k_end[:, grp], (((0,), (0,)), ((), ())),
                              preferred_element_type=F32)
        state_ref[hd] = state * chunk_decay[:, grp] + upd
        outs.append(o)
    return outs


def _finish_heads(outs, gate, gain, dv, o_ref):
    for hd, o in enumerate(outs):
        sl = slice(hd * LANES, (hd + 1) * LANES)
        ms = jnp.sum(o * o, axis=-1, keepdims=True) * (1.0 / dv)
        y = o * lax.rsqrt(ms + RMS_EPS) * gain
        o_ref[:, sl] = (y * _silu(gate[:, sl])).astype(o_ref.dtype)


def _gla_kernel(q_ref, k_ref, v_ref, gate_ref, lr_ref, wd_ref, bd_ref, gain_ref, msum_ref,
                owner_ref, o_ref, state_ref):
    @pl.when(pl.program_id(1) == 0)
    def _():
        state_ref[...] = jnp.zeros_like(state_ref)

    lr = lr_ref[...]
    lr_hi, lr_lo = _split_bf16(lr)
    wd_hi, wd_lo = _split_bf16(wd_ref[...])
    logits = _dot(lr_hi, wd_hi) + _dot(lr_hi, wd_lo) + _dot(lr_lo, wd_hi) + bd_ref[...]
    lf = -_softplus(-logits) * (1.0 / GLA_GATE_NORMALIZER)
    outs = _gated_linear_chunk(q_ref[...] * GLA_DK ** -0.5, k_ref[...], v_ref[...], lf,
                               msum_ref, owner_ref, state_ref, heads=GLA_HEADS,
                               dk_lanes=GLA_DK_PAD)
    _finish_heads(outs, gate_ref[...], gain_ref[...], GLA_DV, o_ref)


def _hg_kernel(layer, q_ref, f_ref, v_ref, gate_ref, lb_ref, gain_ref, msum_ref,
               owner_ref, o_ref, state_ref):
    @pl.when(pl.program_id(1) == 0)
    def _():
        state_ref[...] = jnp.zeros_like(state_ref)

    logits = lb_ref[...]
    ex = jnp.exp(logits - jnp.max(logits, axis=0, keepdims=True))
    probs = ex / jnp.sum(ex, axis=0, keepdims=True)
    lb = jnp.zeros_like(probs[0:1])
    for d in range(1, layer + 1):
        lb = lb + probs[d:d + 1]

    hf = f_ref[...]
    sp = _softplus(-hf)
    log_sig = -sp
    a = jnp.log(jnp.maximum(lb, 1e-30))
    b = jnp.log(1.0 - lb) + log_sig
    lae = jnp.maximum(a, b) + jnp.log(1.0 + jnp.exp(-jnp.abs(a - b)))
    lf = jnp.where(lb > 0.0, lae, log_sig)
    k = (1.0 - lb) * jnp.exp(-(hf + sp))
    outs = _gated_linear_chunk(q_ref[...], k, v_ref[...], lf, msum_ref, owner_ref, state_ref,
                               heads=HG_HEADS, dk_lanes=HG_DK)
    _finish_heads(outs, gate_ref[...], gain_ref[...], HG_DV, o_ref)


def _proj_spec(name, rows, n_chunks):
    off, width = _SEG_OFF[name]
    assert off % width == 0
    return pl.BlockSpec((rows, width), lambda b, c: (b * n_chunks + c, off // width))


def _mixer_call(kernel_fn, proj, seg_names, extras, *, batch, heads, name):
    m = proj.shape[0]
    n_chunks = m // batch // MIX_CHUNK
    msum = jnp.asarray(_decay_sum_matrix(MIX_CHUNK), BF16)
    owner = jnp.asarray(_pair_owner_matrix(MIX_CHUNK))
    in_specs = [_proj_spec(s, MIX_CHUNK, n_chunks) for s in seg_names]
    in_specs += [pl.BlockSpec(e.shape, lambda b, c: (0, 0)) for e in (*extras, msum, owner)]
    width = heads * HEAD_V_PAD
    return pl.pallas_call(
        kernel_fn,
        grid=(batch, n_chunks),
        in_specs=in_specs,
        out_specs=pl.BlockSpec((MIX_CHUNK, width), lambda b, c: (b * n_chunks + c, 0)),
        out_shape=jax.ShapeDtypeStruct((m, width), BF16),
        scratch_shapes=[pltpu.VMEM((heads, HEAD_V_PAD, LANES), F32)],
        compiler_params=pltpu.CompilerParams(
            dimension_semantics=("arbitrary", "arbitrary"), vmem_limit_bytes=VMEM_LIMIT),
        name=name,
    )(*([proj] * len(seg_names)), *extras, msum, owner)


def _pair_rms(x, g, lo_lanes):
    x2 = x * x
    s_lo = jnp.sum(jnp.where(lo_lanes, x2, 0.0), axis=-1, keepdims=True)
    s_hi = jnp.sum(jnp.where(lo_lanes, 0.0, x2), axis=-1, keepdims=True)
    ms = jnp.where(lo_lanes, s_lo, s_hi) * (1.0 / SB_DH)
    return x * lax.rsqrt(ms + RMS_EPS) * g


def _sb_kernel(q_ref, k_ref, v_ref, qg_ref, kg_ref, og_ref, ut_ref, o_ref,
               kn_scr, vt_scr, q2t_scr, c_scr, acc_scr,
               bias_scr, sp_scr, zs_scr, sp0_scr, w_scr):
    grp = pl.program_id(2)
    bk = SB_BLOCK
    gq = q_ref.shape[0]
    n_kb = k_ref.shape[0] // bk
    lo_lanes = lax.broadcasted_iota(jnp.int32, (1, LANES), 1) < SB_DH

    @pl.when(grp == 0)
    def _():
        kn_scr[...] = _pair_rms(k_ref[...], kg_ref[...], lo_lanes).astype(BF16)

        def transpose_block(kb, carry):
            rows = pl.ds(pl.multiple_of(kb * bk, bk), bk)
            vt_scr[kb] = v_ref[rows, :].T.astype(BF16)
            return carry

        lax.fori_loop(0, n_kb, transpose_block, 0)

        key = lax.broadcasted_iota(jnp.int32, (bk, 2 * gq), 0)
        qry = lax.broadcasted_iota(jnp.int32, (bk, 2 * gq), 1) & (gq - 1)
        for j in range(SB_QGROUP):
            bias_scr[j] = jnp.where(key + j * bk < qry, 0.0, MASK_BIAS)

    qn = _pair_rms(q_ref[...], qg_ref[...], lo_lanes) * (SB_DH ** -0.5 * LOG2_E)
    q2t_scr[:, :gq] = jnp.where(lo_lanes, qn, 0.0).T.astype(BF16)
    q2t_scr[:, gq:] = jnp.where(lo_lanes, 0.0, qn).T.astype(BF16)
    acc_scr[...] = jnp.zeros_like(acc_scr)
    c_scr[...] = jnp.zeros_like(c_scr)

    last = grp * SB_QGROUP + SB_QGROUP - 1
    sign_bit = jnp.uint32(0x80000000)

    def stage_scores(p, masked):
        for u in range(SB_UNROLL):
            m = p * SB_UNROLL + u
            kt = kn_scr[pl.ds(pl.multiple_of((last - m) * bk, bk), bk), :]
            z = _dot(kt, q2t_scr[...])
            if masked:
                z = z + bias_scr[SB_QGROUP - 1 - m]
            neg_abs = lax.bitcast_convert_type(lax.bitcast_convert_type(z, jnp.uint32) | sign_bit, F32)
            sp = jnp.maximum(z, 0.0) + jnp.log2(1.0 + jnp.exp2(neg_abs))
            sp_scr[u] = sp.astype(BF16)
            zs_scr[u] = z - sp
            sp0_scr[u] = sp[0:8]

    def stage_weights(p):
        c = c_scr[...]
        for u in range(SB_UNROLL):
            later = _dot(ut_ref[...], sp_scr[u])
            w_scr[u] = jnp.exp2(zs_scr[u] - later - c).astype(BF16)
            c = c + later[0:1] + sp0_scr[u][0:1]
        c_scr[...] = c

    def stage_values(p):
        for u in range(SB_UNROLL):
            vt = vt_scr[last - (p * SB_UNROLL + u)]
            w = w_scr[u]
            acc_scr[0:SB_DH, :] += _dot(vt[0:SB_DH], w[:, :gq])
            acc_scr[SB_DH:, :] += _dot(vt[SB_DH:], w[:, gq:])

    n_masked = SB_QGROUP // SB_UNROLL
    n_steps = (grp + 1) * n_masked
    for p in range(n_masked):
        if p >= 2:
            stage_values(p - 2)
        if p >= 1:
            stage_weights(p - 1)
        stage_scores(p, True)

    def body(p, carry):
        stage_values(p - 2)
        stage_weights(p - 1)
        stage_scores(p, False)
        return carry

    lax.fori_loop(n_masked, n_steps, body, 0)
    stage_values(n_steps - 2)
    stage_weights(n_steps - 1)
    stage_values(n_steps - 1)

    o_ref[...] = _pair_rms(acc_scr[...].T, og_ref[...], lo_lanes).astype(o_ref.dtype)


def _sb_attention(proj, qg, kg, og, *, batch):
    m = proj.shape[0]
    t = m // batch
    bk = SB_BLOCK
    gq = SB_QGROUP * bk
    n_groups = t // gq
    n_pairs = SB_HEADS * SB_DH // LANES
    oq, ok, ov = (_SEG_OFF[s][0] // LANES for s in ("sq", "sk", "sv"))
    j = np.arange(bk)
    ut = jnp.asarray((j[None, :] > j[:, None]).astype(np.float32), BF16)
    pair = lambda g: jnp.tile(g, 2).reshape(1, LANES)
    const = lambda shape: pl.BlockSpec(shape, lambda b, p, i: (0, 0))
    return pl.pallas_call(
        _sb_kernel,
        grid=(batch, n_pairs, n_groups),
        in_specs=[pl.BlockSpec((gq, LANES), lambda b, p, i: (b * n_groups + i, oq + p)),
                  pl.BlockSpec((t, LANES), lambda b, p, i: (b, ok + p)),
                  pl.BlockSpec((t, LANES), lambda b, p, i: (b, ov + p)),
                  const((1, LANES)), const((1, LANES)), const((1, LANES)),
                  const((bk, bk))],
        out_specs=pl.BlockSpec((gq, LANES), lambda b, p, i: (b * n_groups + i, p)),
        out_shape=jax.ShapeDtypeStruct((m, n_pairs * LANES), BF16),
        scratch_shapes=[pltpu.VMEM((t, LANES), BF16),
                        pltpu.VMEM((t // bk, LANES, bk), BF16),
                        pltpu.VMEM((LANES, 2 * gq), BF16),
                        pltpu.VMEM((1, 2 * gq), F32),
                        pltpu.VMEM((LANES, gq), F32),
                        pltpu.VMEM((SB_QGROUP, bk, 2 * gq), F32),
                        pltpu.VMEM((SB_UNROLL, bk, 2 * gq), BF16),
                        pltpu.VMEM((SB_UNROLL, bk, 2 * gq), F32),
                        pltpu.VMEM((SB_UNROLL, 8, 2 * gq), F32),
                        pltpu.VMEM((SB_UNROLL, bk, 2 * gq), BF16)],
        compiler_params=pltpu.CompilerParams(
            dimension_semantics=("arbitrary", "arbitrary", "arbitrary"),
            vmem_limit_bytes=VMEM_LIMIT),
        name="sb_attention",
    )(proj, proj, proj, pair(qg), pair(kg), pair(og), ut)


def kernel(x, norm_mix_g, w_in, gla_w_decay, gla_b_decay, gla_out_g, sb_q_g, sb_k_g, sb_out_g,
           hg_out_g, hg_lb_logits, w_out, norm_ffn_g, w_ffn_up, w_ffn_down):
    batch, seq, d_model = x.shape
    depth = w_in.shape[0]
    x2 = x.reshape(batch * seq, d_model).astype(F32)

    gla_cols = _padded_head_cols(0, GLA_HEADS, GLA_DK, GLA_DK_PAD)
    rows_a = _padded_head_cols(0, GLA_HEADS, GLA_DV, HEAD_V_PAD)
    sb_lo = GLA_HEADS * GLA_DV
    hg_lo = sb_lo + SB_HEADS * SB_DH
    rows_c = _padded_head_cols(hg_lo, HG_HEADS, HG_DV, HEAD_V_PAD)

    for li in range(depth):
        w_in_p = _take_padded(w_in[li], _PROJ_SRC, 1).astype(BF16)
        proj = _norm_matmul(x2, norm_mix_g[li], w_in_p, tm=512, tn=PROJ_COLS // 2, out_dtype=F32)

        wd = _take_padded(gla_w_decay[li].astype(F32), gla_cols, 1)
        wd = jnp.pad(wd, ((0, LANES - GLA_LOWRANK), (0, 0)))
        bd = _take_padded(gla_b_decay[li].astype(F32), gla_cols, 0).reshape(1, -1)
        gain_a = jnp.pad(gla_out_g[li].astype(F32), (0, HEAD_V_PAD - GLA_DV)).reshape(1, -1)
        o_a = _mixer_call(_gla_kernel, proj, ("gq", "gk", "gv", "gg", "glr"), (wd, bd, gain_a),
                          batch=batch, heads=GLA_HEADS, name="gla_mixer")

        o_b = _sb_attention(proj, sb_q_g[li].astype(F32), sb_k_g[li].astype(F32),
                            sb_out_g[li].astype(F32), batch=batch)

        gain_c = jnp.pad(hg_out_g[li].astype(F32), (0, HEAD_V_PAD - HG_DV)).reshape(1, -1)
        o_c = _mixer_call(functools.partial(_hg_kernel, li), proj, ("hq", "hf", "hi", "hg"),
                          (hg_lb_logits.astype(F32), gain_c),
                          batch=batch, heads=HG_HEADS, name="hg_mixer")

        wo = w_out[li]
        w_a = _take_padded(wo, rows_a, 0).astype(BF16)
        w_b = wo[sb_lo:hg_lo].astype(BF16)
        w_c = _take_padded(wo, rows_c, 0).astype(BF16)
        x2 = _residual_matmul(x2, (o_a, o_b, o_c), (w_a, w_b, w_c), tm=1024)

        act = _ffn_up(x2, norm_ffn_g[li], w_ffn_up[li].astype(BF16), tm=512, tn=1408)
        x2 = _residual_matmul(x2, (act,), (w_ffn_down[li].astype(BF16),), tm=512)
    return x2.reshape(batch, seq, d_model).astype(x.dtype)
```

```python
import functools

import numpy as np
import jax
import jax.numpy as jnp
from jax import lax
from jax.experimental import pallas as pl
from jax.experimental.pallas import tpu as pltpu

F32 = jnp.float32
BF16 = jnp.bfloat16

LANES = 128
RMS_EPS = 1e-6

GLA_HEADS, GLA_DK, GLA_DV, GLA_LOWRANK = 4, 48, 96, 16
GLA_GATE_NORMALIZER = 16.0
SB_HEADS, SB_DH, SB_BLOCK = 6, 64, 128
HG_HEADS, HG_DK, HG_DV = 4, 128, 64

GLA_DK_PAD = 64
HEAD_V_PAD = LANES
MIX_CHUNK = 128
SB_QGROUP = 4
SB_UNROLL = 2
SB_LOOP_STEPS = 2
LOG2_E = 1.4426950408889634
MASK_BIAS = -1e30
assert SB_QGROUP % SB_UNROLL == 0 and SB_QGROUP // SB_UNROLL >= 2
assert (SB_QGROUP // SB_UNROLL) % SB_LOOP_STEPS == 0
VMEM_LIMIT = 48 * 1024 * 1024

_MAIN_SEGS = (("gv", 512), ("gg", 512), ("hq", 512), ("hi", 512), ("hg", 512),
              ("gq", 256), ("gk", 256), ("sq", 384), ("sk", 384), ("sv", 384))
_GATE_SEGS = (("hf", 512), ("glr", 128))
_SEG = {}
for _arr, _segs in enumerate((_MAIN_SEGS, _GATE_SEGS)):
    _off = 0
    for _name, _w in _segs:
        _SEG[_name] = (_arr, _off, _w)
        _off += _w
MAIN_COLS = sum(w for _, w in _MAIN_SEGS)
GATE_COLS = sum(w for _, w in _GATE_SEGS)


def _padded_head_cols(start, heads, width, pad):
    idx = -np.ones((heads, pad), np.int64)
    idx[:, :width] = start + np.arange(heads)[:, None] * width + np.arange(width)[None, :]
    return idx.reshape(-1)


def _proj_source_columns():
    sizes = (GLA_HEADS * GLA_DK, GLA_HEADS * GLA_DK, GLA_HEADS * GLA_DV, GLA_LOWRANK,
             GLA_HEADS * GLA_DV, SB_HEADS * SB_DH, SB_HEADS * SB_DH, SB_HEADS * SB_DH,
             HG_HEADS * HG_DK, HG_HEADS * HG_DK, HG_HEADS * HG_DV, HG_HEADS * HG_DV)
    starts = np.concatenate([[0], np.cumsum(sizes)[:-1]])
    (gq, gk, gv, glr, gg, sq, sk, sv, hq, hf, hi, hg) = [int(s) for s in starts]
    src = {
        "gq": _padded_head_cols(gq, GLA_HEADS, GLA_DK, GLA_DK_PAD),
        "gk": _padded_head_cols(gk, GLA_HEADS, GLA_DK, GLA_DK_PAD),
        "gv": _padded_head_cols(gv, GLA_HEADS, GLA_DV, HEAD_V_PAD),
        "gg": _padded_head_cols(gg, GLA_HEADS, GLA_DV, HEAD_V_PAD),
        "glr": _padded_head_cols(glr, 1, GLA_LOWRANK, LANES),
        "sq": np.arange(sq, sq + SB_HEADS * SB_DH),
        "sk": np.arange(sk, sk + SB_HEADS * SB_DH),
        "sv": np.arange(sv, sv + SB_HEADS * SB_DH),
        "hq": np.arange(hq, hq + HG_HEADS * HG_DK),
        "hf": np.arange(hf, hf + HG_HEADS * HG_DK),
        "hi": _padded_head_cols(hi, HG_HEADS, HG_DV, HEAD_V_PAD),
        "hg": _padded_head_cols(hg, HG_HEADS, HG_DV, HEAD_V_PAD),
    }
    return tuple(np.concatenate([src[name] for name, _ in segs]) for segs in (_MAIN_SEGS, _GATE_SEGS))


_MAIN_SRC, _GATE_SRC = _proj_source_columns()


def _take_padded(arr, src, axis):
    taken = jnp.take(arr, jnp.asarray(np.maximum(src, 0)), axis=axis)
    shape = [1] * arr.ndim
    shape[axis] = len(src)
    return jnp.where(jnp.asarray(src >= 0).reshape(shape), taken, 0)


def _dot(a, b):
    return jnp.dot(a, b, preferred_element_type=F32)


def _dot_nt(a, b):
    return lax.dot_general(a, b, (((1,), (1,)), ((), ())), preferred_element_type=F32)


def _split_bf16(x):
    hi = x.astype(BF16)
    lo = (x - hi.astype(F32)).astype(BF16)
    return hi, lo


def _softplus(z):
    return jnp.maximum(z, 0.0) + jnp.log(1.0 + jnp.exp(-jnp.abs(z)))


def _silu(z):
    return z / (1.0 + jnp.exp(-z))


def _rms_rows(x, g):
    ms = jnp.mean(x * x, axis=-1, keepdims=True)
    return x * lax.rsqrt(ms + RMS_EPS) * g


def _in_proj_kernel(x_ref, g_ref, w_ref, wg_ref, o_ref, og_ref, h_scr):
    @pl.when(pl.program_id(1) == 0)
    def _():
        h = _rms_rows(x_ref[...], g_ref[...]).astype(BF16)
        h_scr[...] = h
        og_ref[...] = _dot(h, wg_ref[...])

    o_ref[...] = _dot(h_scr[...], w_ref[...]).astype(o_ref.dtype)


def _in_proj(x, g, w_main, w_gate, *, tm, tn):
    m, d = x.shape
    n, ng = w_main.shape[1], w_gate.shape[1]
    return pl.pallas_call(
        _in_proj_kernel,
        grid=(m // tm, n // tn),
        in_specs=[pl.BlockSpec((tm, d), lambda i, j: (i, 0)),
                  pl.BlockSpec((1, d), lambda i, j: (0, 0)),
                  pl.BlockSpec((d, tn), lambda i, j: (0, j)),
                  pl.BlockSpec((d, ng), lambda i, j: (0, 0))],
        out_specs=[pl.BlockSpec((tm, tn), lambda i, j: (i, j)),
                   pl.BlockSpec((tm, ng), lambda i, j: (i, 0))],
        out_shape=[jax.ShapeDtypeStruct((m, n), BF16), jax.ShapeDtypeStruct((m, ng), F32)],
        scratch_shapes=[pltpu.VMEM((tm, d), BF16)],
        compiler_params=pltpu.CompilerParams(
            dimension_semantics=("arbitrary", "arbitrary"), vmem_limit_bytes=VMEM_LIMIT),
        name="in_proj",
    )(x, g.reshape(1, d), w_main, w_gate)


def _ffn_up_kernel(x_ref, g_ref, wg_ref, wu_ref, o_ref, h_scr):
    @pl.when(pl.program_id(1) == 0)
    def _():
        h_scr[...] = _rms_rows(x_ref[...], g_ref[...]).astype(BF16)

    h = h_scr[...]
    gate = _dot(h, wg_ref[...])
    up = _dot(h, wu_ref[...])
    o_ref[...] = (_silu(gate) * up).astype(o_ref.dtype)


def _ffn_up(x, g, w_up, *, tm, tn):
    m, d = x.shape
    d_ff = w_up.shape[1] // 2
    nj = d_ff // tn
    return pl.pallas_call(
        _ffn_up_kernel,
        grid=(m // tm, nj),
        in_specs=[pl.BlockSpec((tm, d), lambda i, j: (i, 0)),
                  pl.BlockSpec((1, d), lambda i, j: (0, 0)),
                  pl.BlockSpec((d, tn), lambda i, j: (0, j)),
                  pl.BlockSpec((d, tn), lambda i, j: (0, j + nj))],
        out_specs=pl.BlockSpec((tm, tn), lambda i, j: (i, j)),
        out_shape=jax.ShapeDtypeStruct((m, d_ff), BF16),
        scratch_shapes=[pltpu.VMEM((tm, d), BF16)],
        compiler_params=pltpu.CompilerParams(
            dimension_semantics=("arbitrary", "arbitrary"), vmem_limit_bytes=VMEM_LIMIT),
        name="ffn_up",
    )(x, g.reshape(1, d), w_up, w_up)


def _residual_matmul_kernel(n_in, res_ref, *refs):
    a_refs, w_refs, o_ref = refs[:n_in], refs[n_in:2 * n_in], refs[2 * n_in]
    acc = res_ref[...]
    for a_ref, w_ref in zip(a_refs, w_refs):
        acc = acc + _dot(a_ref[...], w_ref[...])
    o_ref[...] = acc


def _residual_matmul(res, acts, weights, *, tm):
    m, n = res.shape
    n_in = len(acts)
    in_specs = [pl.BlockSpec((tm, n), lambda i: (i, 0))]
    in_specs += [pl.BlockSpec((tm, a.shape[1]), lambda i: (i, 0)) for a in acts]
    in_specs += [pl.BlockSpec(w.shape, lambda i: (0, 0)) for w in weights]
    return pl.pallas_call(
        functools.partial(_residual_matmul_kernel, n_in),
        grid=(m // tm,),
        in_specs=in_specs,
        out_specs=pl.BlockSpec((tm, n), lambda i: (i, 0)),
        out_shape=jax.ShapeDtypeStruct((m, n), F32),
        compiler_params=pltpu.CompilerParams(
            dimension_semantics=("arbitrary",), vmem_limit_bytes=VMEM_LIMIT),
        name="residual_matmul",
    )(res, *acts, *weights)


def _decay_sum_matrix(chunk):
    t = np.arange(chunk)
    blocks = [(t[None, :] <= t[:, None]), (t[None, :] > t[:, None])]
    h = 1
    while h < chunk:
        mid = (t // (2 * h)) * (2 * h) + h
        right = t >= mid
        m = np.where(right[:, None],
                     (t[None, :] >= mid[:, None]) & (t[None, :] <= t[:, None]),
                     (t[None, :] > t[:, None]) & (t[None, :] < mid[:, None]))
        blocks.append(m)
        h *= 2
    return np.concatenate(blocks, axis=0).astype(np.float32)


def _pair_owner_matrix(chunk):
    t = np.arange(chunk)
    x = t[:, None] ^ t[None, :]
    level = np.floor(np.log2(np.maximum(x, 1))).astype(np.int32)
    return np.where(x == 0, 0, np.where(t[None, :] < t[:, None], 1 + level, -1)).astype(np.int32)


def _chunk_step(qb, kb, lf, v_ref, gate_ref, gain, first_chunk, msum_ref, owner_ref,
                rd, wr, state_ref, o_ref, *, heads, dk_lanes, dv):
    q_rd, k_rd, dec_rd = rd
    q_wr, k_wr, dec_wr = wr
    c = qb.shape[0]
    n_levels = c.bit_length() - 1
    n_tiles = n_levels + 1
    lf_b = lf.astype(BF16)

    def decay_rows(block):
        return jnp.exp(_dot(msum_ref[block * c:(block + 1) * c, :], lf_b).astype(BF16))

    def store_slot(t):
        if t == 0:
            q_wr[0] = qb
            k_wr[0] = kb.T
        elif t <= n_levels:
            e = decay_rows(1 + t)
            q_wr[t] = qb * e
            k_wr[t] = (kb * e).T
        else:
            q_wr[t] = qb * decay_rows(0)
            k_wr[t] = (kb * decay_rows(1)).T
            total = lax.dot_general(lf_b, jnp.ones((c, LANES), BF16), (((0,), (0,)), ((), ())),
                                    preferred_element_type=F32)
            dec_wr[...] = jnp.exp(total)

    owner = owner_ref[...]
    owned = [owner == t for t in range(n_tiles)]
    lane = lax.broadcasted_iota(jnp.int32, (1, LANES), 1)

    for hd in range(heads):
        g0 = (hd * dk_lanes // LANES) * LANES
        grp = slice(g0, g0 + LANES)
        if dk_lanes < LANES:
            lo = hd * dk_lanes - g0
            head_lanes = jnp.where((lane >= lo) & (lane < lo + dk_lanes), 1.0, 0.0).astype(BF16)
            pick = lambda t: q_rd[t, :, grp] * head_lanes
        else:
            pick = lambda t: q_rd[t, :, grp]
        scores = jnp.zeros((c, c), F32)
        for t in range(n_tiles):
            scores = jnp.where(owned[t], _dot(pick(t), k_rd[t, grp, :]), scores)
        sl = slice(hd * LANES, (hd + 1) * LANES)
        v_h = v_ref[:, sl]
        state = jnp.where(first_chunk, 0.0, state_ref[hd])
        o = _dot(scores.astype(BF16), v_h) + _dot(pick(n_tiles), state.astype(BF16))
        state_ref[hd] = state * dec_rd[grp, :] + _dot(k_rd[n_tiles, grp, :], v_h)
        ms = jnp.sum(o * o, axis=-1, keepdims=True) * (1.0 / dv)
        y = o * lax.rsqrt(ms + RMS_EPS) * gain
        o_ref[:, sl] = (y * _silu(gate_ref[:, sl].astype(F32))).astype(o_ref.dtype)

        for t in range(hd, n_tiles + 1, heads):
            store_slot(t)


def _mixer_step(scratch, step_fn):
    @pl.when(pl.program_id(0) == 0)
    def _():
        for r in scratch:
            r[...] = jnp.zeros_like(r)

    parity = lax.rem(pl.program_id(0), 2)
    for buf in (0, 1):
        @pl.when(parity == buf)
        def _():
            step_fn(1 - buf, buf)


def _gla_kernel(n_chunks, q_ref, k_ref, lr_ref, wd_ref, bd_ref, v_ref, gate_ref, gain_ref,
                msum_ref, owner_ref, o_ref, state_ref, q_scr, k_scr, dec_scr):
    first_chunk = lax.rem(pl.program_id(0) + n_chunks - 1, n_chunks) == 0
    bufs = lambda b: (q_scr.at[b], k_scr.at[b], dec_scr.at[b])

    def step(rd, wr):
        lr_hi, lr_lo = _split_bf16(lr_ref[...])
        wd_hi, wd_lo = _split_bf16(wd_ref[...])
        logits = _dot(lr_hi, wd_hi) + _dot(lr_hi, wd_lo) + _dot(lr_lo, wd_hi) + bd_ref[...]
        lf = -_softplus(-logits) * (1.0 / GLA_GATE_NORMALIZER)
        qb = (q_ref[...].astype(F32) * GLA_DK ** -0.5).astype(BF16)
        _chunk_step(qb, k_ref[...], lf, v_ref, gate_ref, gain_ref[...], first_chunk, msum_ref,
                    owner_ref, bufs(rd), bufs(wr), state_ref, o_ref,
                    heads=GLA_HEADS, dk_lanes=GLA_DK_PAD, dv=GLA_DV)

    _mixer_step((state_ref, q_scr, k_scr, dec_scr), step)


def _hg_kernel(layer, n_chunks, q_ref, f_ref, lb_ref, v_ref, gate_ref, gain_ref,
               msum_ref, owner_ref, o_ref, state_ref, q_scr, k_scr, dec_scr):
    first_chunk = lax.rem(pl.program_id(0) + n_chunks - 1, n_chunks) == 0
    bufs = lambda b: (q_scr.at[b], k_scr.at[b], dec_scr.at[b])

    def step(rd, wr):
        logits = lb_ref[...]
        ex = jnp.exp(logits - jnp.max(logits, axis=0, keepdims=True))
        probs = ex / jnp.sum(ex, axis=0, keepdims=True)
        lb = jnp.zeros_like(probs[0:1])
        for d in range(1, layer + 1):
            lb = lb + probs[d:d + 1]

        hf = f_ref[...]
        sp = _softplus(-hf)
        log_sig = -sp
        a = jnp.log(jnp.maximum(lb, 1e-30))
        b = jnp.log(1.0 - lb) + log_sig
        lae = jnp.maximum(a, b) + jnp.log(1.0 + jnp.exp(-jnp.abs(a - b)))
        lf = jnp.where(lb > 0.0, lae, log_sig)
        k = (1.0 - lb) * jnp.exp(-(hf + sp))
        _chunk_step(q_ref[...], k.astype(BF16), lf, v_ref, gate_ref, gain_ref[...], first_chunk,
                    msum_ref, owner_ref, bufs(rd), bufs(wr), state_ref, o_ref,
                    heads=HG_HEADS, dk_lanes=HG_DK, dv=HG_DV)

    _mixer_step((state_ref, q_scr, k_scr, dec_scr), step)


def _mixer_call(kernel_fn, projs, lead_segs, lead_extras, lag_segs, lag_extras,
                *, batch, heads, dk_lanes, name):
    m = projs[0].shape[0]
    n_chunks = m // batch // MIX_CHUNK
    n_steps = m // MIX_CHUNK
    msum = jnp.asarray(_decay_sum_matrix(MIX_CHUNK), BF16)
    owner = jnp.asarray(_pair_owner_matrix(MIX_CHUNK))
    lead = lambda s: jnp.minimum(s, n_steps - 1)
    lag = lambda s: jnp.maximum(s - 1, 0)

    def seg_spec(seg, row_of):
        _, off, width = _SEG[seg]
        assert off % width == 0
        return pl.BlockSpec((MIX_CHUNK, width), lambda s: (row_of(s), off // width))

    const = lambda e: pl.BlockSpec(e.shape, lambda s: (0, 0))
    in_specs = [seg_spec(seg, lead) for seg in lead_segs] + [const(e) for e in lead_extras]
    in_specs += [seg_spec(seg, lag) for seg in lag_segs] + [const(e) for e in lag_extras]
    in_specs += [const(msum), const(owner)]
    seg_arrays = lambda segs: [projs[_SEG[seg][0]] for seg in segs]
    width = heads * HEAD_V_PAD
    n_slots = MIX_CHUNK.bit_length() + 1
    return pl.pallas_call(
        functools.partial(kernel_fn, n_chunks),
        grid=(n_steps + 1,),
        in_specs=in_specs,
        out_specs=pl.BlockSpec((MIX_CHUNK, width), lambda s: (lag(s), 0)),
        out_shape=jax.ShapeDtypeStruct((m, width), BF16),
        scratch_shapes=[pltpu.VMEM((heads, LANES, HEAD_V_PAD), F32),
                        pltpu.VMEM((2, n_slots, MIX_CHUNK, heads * dk_lanes), BF16),
                        pltpu.VMEM((2, n_slots, heads * dk_lanes, MIX_CHUNK), BF16),
                        pltpu.VMEM((2, heads * dk_lanes, LANES), F32)],
        compiler_params=pltpu.CompilerParams(
            dimension_semantics=("arbitrary",), vmem_limit_bytes=VMEM_LIMIT),
        name=name,
    )(*seg_arrays(lead_segs), *lead_extras, *seg_arrays(lag_segs), *lag_extras, msum, owner)


def _pair_rms(x, g, lo_lanes):
    x2 = x * x
    s_lo = jnp.sum(jnp.where(lo_lanes, x2, 0.0), axis=-1, keepdims=True)
    s_hi = jnp.sum(jnp.where(lo_lanes, 0.0, x2), axis=-1, keepdims=True)
    ms = jnp.where(lo_lanes, s_lo, s_hi) * (1.0 / SB_DH)
    return x * lax.rsqrt(ms + RMS_EPS) * g


def _sb_kernel(q_ref, k_ref, v_ref, qg_ref, kg_ref, og_ref, ut_ref, o_ref,
               kn_scr, vt_scr, q2t_scr, c_scr, acc_scr,
               bias_scr, sp_scr, zs_scr, sp0_scr, w_scr):
    grp = pl.program_id(2)
    bk = SB_BLOCK
    gq = q_ref.shape[0]
    n_kb = k_ref.shape[0] // bk
    lo_lanes = lax.broadcasted_iota(jnp.int32, (1, LANES), 1) < SB_DH

    @pl.when(grp == 0)
    def _():
        kn_scr[...] = _pair_rms(k_ref[...].astype(F32), kg_ref[...], lo_lanes).astype(BF16)

        def transpose_block(kb, carry):
            rows = pl.ds(pl.multiple_of(kb * bk, bk), bk)
            vt_scr[kb] = v_ref[rows, :].astype(F32).T.astype(BF16)
            return carry

        lax.fori_loop(0, n_kb, transpose_block, 0)

        key = lax.broadcasted_iota(jnp.int32, (bk, 2 * gq), 0)
        qry = lax.broadcasted_iota(jnp.int32, (bk, 2 * gq), 1) & (gq - 1)
        for j in range(SB_QGROUP):
            bias_scr[j] = jnp.where(key + j * bk < qry, 0.0, MASK_BIAS)

    qn = _pair_rms(q_ref[...].astype(F32), qg_ref[...], lo_lanes) * (SB_DH ** -0.5 * LOG2_E)
    q2t_scr[:, :gq] = jnp.where(lo_lanes, qn, 0.0).T.astype(BF16)
    q2t_scr[:, gq:] = jnp.where(lo_lanes, 0.0, qn).T.astype(BF16)
    acc_scr[...] = jnp.zeros_like(acc_scr)
    c_scr[...] = jnp.zeros_like(c_scr)

    last = grp * SB_QGROUP + SB_QGROUP - 1

    def stage_scores(p, masked):
        for u in range(SB_UNROLL):
            m = p * SB_UNROLL + u
            kt = kn_scr[pl.ds(pl.multiple_of((last - m) * bk, bk), bk), :]
            z = _dot(kt, q2t_scr[...])
            if masked:
                z = z + bias_scr[SB_QGROUP - 1 - m]
            sp = jnp.maximum(z, 0.0) + jnp.log2(1.0 + jnp.exp2(-jnp.abs(z)))
            sp_scr[u] = sp.astype(BF16)
            zs_scr[u] = z - sp
            sp0_scr[u] = sp[0:8]

    def stage_weights(p):
        c = c_scr[...]
        for u in range(SB_UNROLL):
            later = _dot(ut_ref[...], sp_scr[u])
            w_scr[u] = jnp.exp2(zs_scr[u] - later - c).astype(BF16)
            c = c + later[0:1] + sp0_scr[u][0:1]
        c_scr[...] = c

    def stage_values(p):
        for u in range(SB_UNROLL):
            vt = vt_scr[last - (p * SB_UNROLL + u)]
            w = w_scr[u]
            acc_scr[0:SB_DH, :] += _dot(vt[0:SB_DH], w[:, :gq])
            acc_scr[SB_DH:, :] += _dot(vt[SB_DH:], w[:, gq:])

    n_masked = SB_QGROUP // SB_UNROLL
    n_steps = (grp + 1) * n_masked
    for p in range(n_masked):
        if p >= 2:
            stage_values(p - 2)
        if p >= 1:
            stage_weights(p - 1)
        stage_scores(p, True)

    def body(i, carry):
        for s in range(SB_LOOP_STEPS):
            p = n_masked + i * SB_LOOP_STEPS + s
            stage_values(p - 2)
            stage_weights(p - 1)
            stage_scores(p, False)
        return carry

    lax.fori_loop(0, grp * (n_masked // SB_LOOP_STEPS), body, 0)
    stage_values(n_steps - 2)
    stage_weights(n_steps - 1)
    stage_values(n_steps - 1)

    o_ref[...] = _pair_rms(acc_scr[...].T, og_ref[...], lo_lanes).astype(o_ref.dtype)


def _sb_attention(proj, qg, kg, og, *, batch):
    m = proj.shape[0]
    t = m // batch
    bk = SB_BLOCK
    gq = SB_QGROUP * bk
    n_groups = t // gq
    n_pairs = SB_HEADS * SB_DH // LANES
    oq, ok, ov = (_SEG[s][1] // LANES for s in ("sq", "sk", "sv"))
    j = np.arange(bk)
    ut = jnp.asarray((j[None, :] > j[:, None]).astype(np.float32), BF16)
    pair = lambda g: jnp.tile(g, 2).reshape(1, LANES)
    const = lambda shape: pl.BlockSpec(shape, lambda b, p, i: (0, 0))
    return pl.pallas_call(
        _sb_kernel,
        grid=(batch, n_pairs, n_groups),
        in_specs=[pl.BlockSpec((gq, LANES), lambda b, p, i: (b * n_groups + i, oq + p)),
                  pl.BlockSpec((t, LANES), lambda b, p, i: (b, ok + p)),
                  pl.BlockSpec((t, LANES), lambda b, p, i: (b, ov + p)),
                  const((1, LANES)), const((1, LANES)), const((1, LANES)),
                  const((bk, bk))],
        out_specs=pl.BlockSpec((gq, LANES), lambda b, p, i: (b * n_groups + i, p)),
        out_shape=jax.ShapeDtypeStruct((m, n_pairs * LANES), BF16),
        scratch_shapes=[pltpu.VMEM((t, LANES), BF16),
                        pltpu.VMEM((t // bk, LANES, bk), BF16),
                        pltpu.VMEM((LANES, 2 * gq), BF16),
                        pltpu.VMEM((1, 2 * gq), F32),
                        pltpu.VMEM((LANES, gq), F32),
                        pltpu.VMEM((SB_QGROUP, bk, 2 * gq), F32),
                        pltpu.VMEM((SB_UNROLL, bk, 2 * gq), BF16),
                        pltpu.VMEM((SB_UNROLL, bk, 2 * gq), F32),
                        pltpu.VMEM((SB_UNROLL, 8, 2 * gq), F32),
                        pltpu.VMEM((SB_UNROLL, bk, 2 * gq), BF16)],
        compiler_params=pltpu.CompilerParams(
            dimension_semantics=("arbitrary", "arbitrary", "arbitrary"),
            vmem_limit_bytes=VMEM_LIMIT),
        name="sb_attention",
    )(proj, proj, proj, pair(qg), pair(kg), pair(og), ut)


def kernel(x, norm_mix_g, w_in, gla_w_decay, gla_b_decay, gla_out_g, sb_q_g, sb_k_g, sb_out_g,
           hg_out_g, hg_lb_logits, w_out, norm_ffn_g, w_ffn_up, w_ffn_down):
    batch, seq, d_model = x.shape
    depth = w_in.shape[0]
    x2 = x.reshape(batch * seq, d_model).astype(F32)

    gla_cols = _padded_head_cols(0, GLA_HEADS, GLA_DK, GLA_DK_PAD)
    rows_a = _padded_head_cols(0, GLA_HEADS, GLA_DV, HEAD_V_PAD)
    sb_lo = GLA_HEADS * GLA_DV
    hg_lo = sb_lo + SB_HEADS * SB_DH
    rows_c = _padded_head_cols(hg_lo, HG_HEADS, HG_DV, HEAD_V_PAD)

    for li in range(depth):
        w_main = _take_padded(w_in[li], _MAIN_SRC, 1).astype(BF16)
        w_gate = _take_padded(w_in[li], _GATE_SRC, 1).astype(BF16)
        projs = _in_proj(x2, norm_mix_g[li], w_main, w_gate, tm=1024, tn=MAIN_COLS // 3)

        wd = _take_padded(gla_w_decay[li].astype(F32), gla_cols, 1)
        wd = jnp.pad(wd, ((0, LANES - GLA_LOWRANK), (0, 0)))
        bd = _take_padded(gla_b_decay[li].astype(F32), gla_cols, 0).reshape(1, -1)
        gain_a = jnp.pad(gla_out_g[li].astype(F32), (0, HEAD_V_PAD - GLA_DV)).reshape(1, -1)
        o_a = _mixer_call(_gla_kernel, projs, ("gq", "gk", "glr"), (wd, bd),
                          ("gv", "gg"), (gain_a,), batch=batch, heads=GLA_HEADS,
                          dk_lanes=GLA_DK_PAD, name="gla_mixer")

        o_b = _sb_attention(projs[0], sb_q_g[li].astype(F32), sb_k_g[li].astype(F32),
                            sb_out_g[li].astype(F32), batch=batch)

        gain_c = jnp.pad(hg_out_g[li].astype(F32), (0, HEAD_V_PAD - HG_DV)).reshape(1, -1)
        o_c = _mixer_call(functools.partial(_hg_kernel, li), projs, ("hq", "hf"),
                          (hg_lb_logits.astype(F32),), ("hi", "hg"), (gain_c,), batch=batch,
                          heads=HG_HEADS, dk_lanes=HG_DK, name="hg_mixer")

        wo = w_out[li]
        w_a = _take_padded(wo, rows_a, 0).astype(BF16)
        w_b = wo[sb_lo:hg_lo].astype(BF16)
        w_c = _take_padded(wo, rows_c, 0).astype(BF16)
        x2 = _residual_matmul(x2, (o_a, o_b, o_c), (w_a, w_b, w_c), tm=1024)

        act = _ffn_up(x2, norm_ffn_g[li], w_ffn_up[li].astype(BF16), tm=1024, tn=1408)
        x2 = _residual_matmul(x2, (act,), (w_ffn_down[li].astype(BF16),), tm=512)
    return x2.reshape(batch, seq, d_model).astype(x.dtype)
```

```python
import functools

import numpy as np
import jax
import jax.numpy as jnp
from jax import lax
from jax.experimental import pallas as pl
from jax.experimental.pallas import tpu as pltpu

F32 = jnp.float32
BF16 = jnp.bfloat16

LANES = 128
RMS_EPS = 1e-6

GLA_HEADS, GLA_DK, GLA_DV, GLA_LOWRANK = 4, 48, 96, 16
GLA_GATE_NORMALIZER = 16.0
SB_HEADS, SB_DH, SB_BLOCK = 6, 64, 128
HG_HEADS, HG_DK, HG_DV = 4, 128, 64

GLA_DK_PAD = 64
HEAD_V_PAD = LANES
MIX_CHUNK = 128
SB_QGROUP = 4
SB_UNROLL = 2
SB_LOOP_STEPS = 1
SB_DEAD_LOG2 = 150.0
LOG2_E = 1.4426950408889634
MASK_BIAS = -1e30
assert SB_QGROUP % SB_UNROLL == 0 and SB_QGROUP // SB_UNROLL >= 2
assert (SB_QGROUP // SB_UNROLL) % SB_LOOP_STEPS == 0
VMEM_LIMIT = 48 * 1024 * 1024

_MAIN_SEGS = (("gv", 512), ("gg", 512), ("hq", 512), ("hi", 512), ("hg", 512),
              ("gq", 256), ("gk", 256), ("sq", 384), ("sk", 384), ("sv", 384))
_GATE_SEGS = (("hf", 512), ("glr", 128))
_SEG = {}
for _arr, _segs in enumerate((_MAIN_SEGS, _GATE_SEGS)):
    _off = 0
    for _name, _w in _segs:
        _SEG[_name] = (_arr, _off, _w)
        _off += _w
MAIN_COLS = sum(w for _, w in _MAIN_SEGS)
GATE_COLS = sum(w for _, w in _GATE_SEGS)


def _padded_head_cols(start, heads, width, pad):
    idx = -np.ones((heads, pad), np.int64)
    idx[:, :width] = start + np.arange(heads)[:, None] * width + np.arange(width)[None, :]
    return idx.reshape(-1)


def _proj_source_columns():
    sizes = (GLA_HEADS * GLA_DK, GLA_HEADS * GLA_DK, GLA_HEADS * GLA_DV, GLA_LOWRANK,
             GLA_HEADS * GLA_DV, SB_HEADS * SB_DH, SB_HEADS * SB_DH, SB_HEADS * SB_DH,
             HG_HEADS * HG_DK, HG_HEADS * HG_DK, HG_HEADS * HG_DV, HG_HEADS * HG_DV)
    starts = np.concatenate([[0], np.cumsum(sizes)[:-1]])
    (gq, gk, gv, glr, gg, sq, sk, sv, hq, hf, hi, hg) = [int(s) for s in starts]
    src = {
        "gq": _padded_head_cols(gq, GLA_HEADS, GLA_DK, GLA_DK_PAD),
        "gk": _padded_head_cols(gk, GLA_HEADS, GLA_DK, GLA_DK_PAD),
        "gv": _padded_head_cols(gv, GLA_HEADS, GLA_DV, HEAD_V_PAD),
        "gg": _padded_head_cols(gg, GLA_HEADS, GLA_DV, HEAD_V_PAD),
        "glr": _padded_head_cols(glr, 1, GLA_LOWRANK, LANES),
        "sq": np.arange(sq, sq + SB_HEADS * SB_DH),
        "sk": np.arange(sk, sk + SB_HEADS * SB_DH),
        "sv": np.arange(sv, sv + SB_HEADS * SB_DH),
        "hq": np.arange(hq, hq + HG_HEADS * HG_DK),
        "hf": np.arange(hf, hf + HG_HEADS * HG_DK),
        "hi": _padded_head_cols(hi, HG_HEADS, HG_DV, HEAD_V_PAD),
        "hg": _padded_head_cols(hg, HG_HEADS, HG_DV, HEAD_V_PAD),
    }
    return tuple(np.concatenate([src[name] for name, _ in segs]) for segs in (_MAIN_SEGS, _GATE_SEGS))


_MAIN_SRC, _GATE_SRC = _proj_source_columns()


def _take_padded(arr, src, axis):
    pad = src < 0
    breaks = np.flatnonzero(np.where(pad[1:] | pad[:-1], pad[1:] != pad[:-1], np.diff(src) != 1)) + 1
    pieces = []
    for run in np.split(src, breaks):
        if run[0] < 0:
            shape = list(arr.shape)
            shape[axis] = len(run)
            pieces.append(jnp.zeros(shape, arr.dtype))
        else:
            pieces.append(lax.slice_in_dim(arr, int(run[0]), int(run[-1]) + 1, axis=axis))
    return jnp.concatenate(pieces, axis=axis)


def _dot(a, b):
    return jnp.dot(a, b, preferred_element_type=F32)


def _dot_nt(a, b):
    return lax.dot_general(a, b, (((1,), (1,)), ((), ())), preferred_element_type=F32)


def _split_bf16(x):
    hi = x.astype(BF16)
    lo = (x - hi.astype(F32)).astype(BF16)
    return hi, lo


def _softplus(z):
    return jnp.maximum(z, 0.0) + jnp.log(1.0 + jnp.exp(-jnp.abs(z)))


def _silu(z):
    return z / (1.0 + jnp.exp(-z))


def _rms_rows(x, g):
    ms = jnp.mean(x * x, axis=-1, keepdims=True)
    return x * lax.rsqrt(ms + RMS_EPS) * g


def _in_proj_kernel(x_ref, g_ref, w_ref, wg_ref, o_ref, og_ref, h_scr):
    @pl.when(pl.program_id(1) == 0)
    def _():
        h = _rms_rows(x_ref[...], g_ref[...]).astype(BF16)
        h_scr[...] = h
        og_ref[...] = _dot(h, wg_ref[...])

    o_ref[...] = _dot(h_scr[...], w_ref[...]).astype(o_ref.dtype)


def _in_proj(x, g, w_main, w_gate, *, tm, tn):
    m, d = x.shape
    n, ng = w_main.shape[1], w_gate.shape[1]
    return pl.pallas_call(
        _in_proj_kernel,
        grid=(m // tm, n // tn),
        in_specs=[pl.BlockSpec((tm, d), lambda i, j: (i, 0)),
                  pl.BlockSpec((1, d), lambda i, j: (0, 0)),
                  pl.BlockSpec((d, tn), lambda i, j: (0, j)),
                  pl.BlockSpec((d, ng), lambda i, j: (0, 0))],
        out_specs=[pl.BlockSpec((tm, tn), lambda i, j: (i, j)),
                   pl.BlockSpec((tm, ng), lambda i, j: (i, 0))],
        out_shape=[jax.ShapeDtypeStruct((m, n), BF16), jax.ShapeDtypeStruct((m, ng), F32)],
        scratch_shapes=[pltpu.VMEM((tm, d), BF16)],
        compiler_params=pltpu.CompilerParams(
            dimension_semantics=("arbitrary", "arbitrary"), vmem_limit_bytes=VMEM_LIMIT),
        name="in_proj",
    )(x, g.reshape(1, d), w_main, w_gate)


def _ffn_up_kernel(x_ref, g_ref, wg_ref, wu_ref, o_ref, h_scr):
    @pl.when(pl.program_id(1) == 0)
    def _():
        h_scr[...] = _rms_rows(x_ref[...], g_ref[...]).astype(BF16)

    h = h_scr[...]
    gate = _dot(h, wg_ref[...])
    up = _dot(h, wu_ref[...])
    o_ref[...] = (_silu(gate) * up).astype(o_ref.dtype)


def _ffn_up(x, g, w_up, *, tm, tn):
    m, d = x.shape
    d_ff = w_up.shape[1] // 2
    nj = d_ff // tn
    return pl.pallas_call(
        _ffn_up_kernel,
        grid=(m // tm, nj),
        in_specs=[pl.BlockSpec((tm, d), lambda i, j: (i, 0)),
                  pl.BlockSpec((1, d), lambda i, j: (0, 0)),
                  pl.BlockSpec((d, tn), lambda i, j: (0, j)),
                  pl.BlockSpec((d, tn), lambda i, j: (0, j + nj))],
        out_specs=pl.BlockSpec((tm, tn), lambda i, j: (i, j)),
        out_shape=jax.ShapeDtypeStruct((m, d_ff), BF16),
        scratch_shapes=[pltpu.VMEM((tm, d), BF16)],
        compiler_params=pltpu.CompilerParams(
            dimension_semantics=("arbitrary", "arbitrary"), vmem_limit_bytes=VMEM_LIMIT),
        name="ffn_up",
    )(x, g.reshape(1, d), w_up, w_up)


def _residual_matmul_kernel(n_in, res_ref, *refs):
    a_refs, w_refs, o_ref = refs[:n_in], refs[n_in:2 * n_in], refs[2 * n_in]
    acc = res_ref[...]
    for a_ref, w_ref in zip(a_refs, w_refs):
        acc = acc + _dot(a_ref[...], w_ref[...])
    o_ref[...] = acc


def _residual_matmul(res, acts, weights, *, tm):
    m, n = res.shape
    n_in = len(acts)
    in_specs = [pl.BlockSpec((tm, n), lambda i: (i, 0))]
    in_specs += [pl.BlockSpec((tm, a.shape[1]), lambda i: (i, 0)) for a in acts]
    in_specs += [pl.BlockSpec(w.shape, lambda i: (0, 0)) for w in weights]
    return pl.pallas_call(
        functools.partial(_residual_matmul_kernel, n_in),
        grid=(m // tm,),
        in_specs=in_specs,
        out_specs=pl.BlockSpec((tm, n), lambda i: (i, 0)),
        out_shape=jax.ShapeDtypeStruct((m, n), F32),
        compiler_params=pltpu.CompilerParams(
            dimension_semantics=("arbitrary",), vmem_limit_bytes=VMEM_LIMIT),
        name="residual_matmul",
    )(res, *acts, *weights)


def _decay_sum_matrix(chunk):
    t = np.arange(chunk)
    blocks = [(t[None, :] <= t[:, None]), (t[None, :] > t[:, None])]
    h = 1
    while h < chunk:
        mid = (t // (2 * h)) * (2 * h) + h
        right = t >= mid
        m = np.where(right[:, None],
                     (t[None, :] >= mid[:, None]) & (t[None, :] <= t[:, None]),
                     (t[None, :] > t[:, None]) & (t[None, :] < mid[:, None]))
        blocks.append(m)
        h *= 2
    return np.concatenate(blocks, axis=0).astype(np.float32)


def _pair_owner_matrix(chunk):
    t = np.arange(chunk)
    x = t[:, None] ^ t[None, :]
    level = np.floor(np.log2(np.maximum(x, 1))).astype(np.int32)
    return np.where(x == 0, 0, np.where(t[None, :] < t[:, None], 1 + level, -1)).astype(np.int32)


def _chunk_step(qb, kb, lf, v_ref, gate_ref, gain, first_chunk, msum_ref, owner_ref,
                rd, wr, state_ref, o_ref, *, heads, dk_lanes, dv):
    q_rd, k_rd, dec_rd = rd
    q_wr, k_wr, dec_wr = wr
    c = qb.shape[0]
    n_levels = c.bit_length() - 1
    n_tiles = n_levels + 1
    lf_b = lf.astype(BF16)

    def decay_rows(block):
        return jnp.exp(_dot(msum_ref[block * c:(block + 1) * c, :], lf_b).astype(BF16))

    def store_slot(t):
        if t == 0:
            q_wr[0] = qb
            k_wr[0] = kb.T
        elif t <= n_levels:
            e = decay_rows(1 + t)
            q_wr[t] = qb * e
            k_wr[t] = (kb * e).T
        else:
            q_wr[t] = qb * decay_rows(0)
            k_wr[t] = (kb * decay_rows(1)).T
            total = lax.dot_general(lf_b, jnp.ones((c, LANES), BF16), (((0,), (0,)), ((), ())),
                                    preferred_element_type=F32)
            dec_wr[...] = jnp.exp(total)

    owner = owner_ref[...]
    owned = [owner == t for t in range(n_tiles)]
    lane = lax.broadcasted_iota(jnp.int32, (1, LANES), 1)

    for hd in range(heads):
        g0 = (hd * dk_lanes // LANES) * LANES
        grp = slice(g0, g0 + LANES)
        if dk_lanes < LANES:
            lo = hd * dk_lanes - g0
            head_lanes = jnp.where((lane >= lo) & (lane < lo + dk_lanes), 1.0, 0.0).astype(BF16)
            pick = lambda t: q_rd[t, :, grp] * head_lanes
        else:
            pick = lambda t: q_rd[t, :, grp]
        scores = jnp.zeros((c, c), F32)
        for t in range(n_tiles):
            scores = jnp.where(owned[t], _dot(pick(t), k_rd[t, grp, :]), scores)
        sl = slice(hd * LANES, (hd + 1) * LANES)
        v_h = v_ref[:, sl]
        state = jnp.where(first_chunk, 0.0, state_ref[hd])
        o = _dot(scores.astype(BF16), v_h) + _dot(pick(n_tiles), state.astype(BF16))
        state_ref[hd] = state * dec_rd[grp, :] + _dot(k_rd[n_tiles, grp, :], v_h)
        ms = jnp.sum(o * o, axis=-1, keepdims=True) * (1.0 / dv)
        y = o * lax.rsqrt(ms + RMS_EPS) * gain
        o_ref[:, sl] = (y * _silu(gate_ref[:, sl].astype(F32))).astype(o_ref.dtype)

        for t in range(hd, n_tiles + 1, heads):
            store_slot(t)


def _mixer_step(scratch, step_fn):
    @pl.when(pl.program_id(0) == 0)
    def _():
        for r in scratch:
            r[...] = jnp.zeros_like(r)

    parity = lax.rem(pl.program_id(0), 2)
    for buf in (0, 1):
        @pl.when(parity == buf)
        def _():
            step_fn(1 - buf, buf)


def _gla_kernel(n_chunks, q_ref, k_ref, lr_ref, wd_ref, bd_ref, v_ref, gate_ref, gain_ref,
                msum_ref, owner_ref, o_ref, state_ref, q_scr, k_scr, dec_scr):
    first_chunk = lax.rem(pl.program_id(0) + n_chunks - 1, n_chunks) == 0
    bufs = lambda b: (q_scr.at[b], k_scr.at[b], dec_scr.at[b])

    def step(rd, wr):
        lr_hi, lr_lo = _split_bf16(lr_ref[...])
        wd_hi, wd_lo = _split_bf16(wd_ref[...])
        logits = _dot(lr_hi, wd_hi) + _dot(lr_hi, wd_lo) + _dot(lr_lo, wd_hi) + bd_ref[...]
        lf = -_softplus(-logits) * (1.0 / GLA_GATE_NORMALIZER)
        qb = (q_ref[...].astype(F32) * GLA_DK ** -0.5).astype(BF16)
        _chunk_step(qb, k_ref[...], lf, v_ref, gate_ref, gain_ref[...], first_chunk, msum_ref,
                    owner_ref, bufs(rd), bufs(wr), state_ref, o_ref,
                    heads=GLA_HEADS, dk_lanes=GLA_DK_PAD, dv=GLA_DV)

    _mixer_step((state_ref, q_scr, k_scr, dec_scr), step)


def _hg_kernel(layer, n_chunks, q_ref, f_ref, lb_ref, v_ref, gate_ref, gain_ref,
               msum_ref, owner_ref, o_ref, state_ref, q_scr, k_scr, dec_scr):
    first_chunk = lax.rem(pl.program_id(0) + n_chunks - 1, n_chunks) == 0
    bufs = lambda b: (q_scr.at[b], k_scr.at[b], dec_scr.at[b])

    def step(rd, wr):
        logits = lb_ref[...]
        ex = jnp.exp(logits - jnp.max(logits, axis=0, keepdims=True))
        probs = ex / jnp.sum(ex, axis=0, keepdims=True)
        lb = jnp.zeros_like(probs[0:1])
        for d in range(1, layer + 1):
            lb = lb + probs[d:d + 1]

        hf = f_ref[...]
        sp = _softplus(-hf)
        log_sig = -sp
        a = jnp.log(jnp.maximum(lb, 1e-30))
        b = jnp.log(1.0 - lb) + log_sig
        lae = jnp.maximum(a, b) + jnp.log(1.0 + jnp.exp(-jnp.abs(a - b)))
        lf = jnp.where(lb > 0.0, lae, log_sig)
        k = (1.0 - lb) * jnp.exp(-(hf + sp))
        _chunk_step(q_ref[...], k.astype(BF16), lf, v_ref, gate_ref, gain_ref[...], first_chunk,
                    msum_ref, owner_ref, bufs(rd), bufs(wr), state_ref, o_ref,
                    heads=HG_HEADS, dk_lanes=HG_DK, dv=HG_DV)

    _mixer_step((state_ref, q_scr, k_scr, dec_scr), step)


def _mixer_call(kernel_fn, projs, lead_segs, lead_extras, lag_segs, lag_extras,
                *, batch, heads, dk_lanes, name):
    m = projs[0].shape[0]
    n_chunks = m // batch // MIX_CHUNK
    n_steps = m // MIX_CHUNK
    msum = jnp.asarray(_decay_sum_matrix(MIX_CHUNK), BF16)
    owner = jnp.asarray(_pair_owner_matrix(MIX_CHUNK))
    lead = lambda s: jnp.minimum(s, n_steps - 1)
    lag = lambda s: jnp.maximum(s - 1, 0)

    def seg_spec(seg, row_of):
        _, off, width = _SEG[seg]
        assert off % width == 0
        return pl.BlockSpec((MIX_CHUNK, width), lambda s: (row_of(s), off // width))

    const = lambda e: pl.BlockSpec(e.shape, lambda s: (0, 0))
    in_specs = [seg_spec(seg, lead) for seg in lead_segs] + [const(e) for e in lead_extras]
    in_specs += [seg_spec(seg, lag) for seg in lag_segs] + [const(e) for e in lag_extras]
    in_specs += [const(msum), const(owner)]
    seg_arrays = lambda segs: [projs[_SEG[seg][0]] for seg in segs]
    width = heads * HEAD_V_PAD
    n_slots = MIX_CHUNK.bit_length() + 1
    return pl.pallas_call(
        functools.partial(kernel_fn, n_chunks),
        grid=(n_steps + 1,),
        in_specs=in_specs,
        out_specs=pl.BlockSpec((MIX_CHUNK, width), lambda s: (lag(s), 0)),
        out_shape=jax.ShapeDtypeStruct((m, width), BF16),
        scratch_shapes=[pltpu.VMEM((heads, LANES, HEAD_V_PAD), F32),
                        pltpu.VMEM((2, n_slots, MIX_CHUNK, heads * dk_lanes), BF16),
                        pltpu.VMEM((2, n_slots, heads * dk_lanes, MIX_CHUNK), BF16),
                        pltpu.VMEM((2, heads * dk_lanes, LANES), F32)],
        compiler_params=pltpu.CompilerParams(
            dimension_semantics=("arbitrary",), vmem_limit_bytes=VMEM_LIMIT),
        name=name,
    )(*seg_arrays(lead_segs), *lead_extras, *seg_arrays(lag_segs), *lag_extras, msum, owner)


def _pair_rms(x, g, lo_lanes):
    x2 = x * x
    s_lo = jnp.sum(jnp.where(lo_lanes, x2, 0.0), axis=-1, keepdims=True)
    s_hi = jnp.sum(jnp.where(lo_lanes, 0.0, x2), axis=-1, keepdims=True)
    ms = jnp.where(lo_lanes, s_lo, s_hi) * (1.0 / SB_DH)
    return x * lax.rsqrt(ms + RMS_EPS) * g


def _sb_kernel(q_ref, k_ref, v_ref, qg_ref, kg_ref, og_ref, ut_ref, o_ref,
               kn_scr, vt_scr, q2t_scr, c_scr, acc_scr,
               bias_scr, sp_scr, zs_scr, sp0_scr, w_scr):
    grp = pl.program_id(2)
    bk = SB_BLOCK
    gq = q_ref.shape[0]
    n_kb = k_ref.shape[0] // bk
    lo_lanes = lax.broadcasted_iota(jnp.int32, (1, LANES), 1) < SB_DH

    @pl.when(grp == 0)
    def _():
        kn_scr[...] = _pair_rms(k_ref[...].astype(F32), kg_ref[...], lo_lanes).astype(BF16)

        def transpose_block(kb, carry):
            rows = pl.ds(pl.multiple_of(kb * bk, bk), bk)
            vt_scr[kb] = v_ref[rows, :].astype(F32).T.astype(BF16)
            return carry

        lax.fori_loop(0, n_kb, transpose_block, 0)

        key = lax.broadcasted_iota(jnp.int32, (bk, 2 * gq), 0)
        qry = lax.broadcasted_iota(jnp.int32, (bk, 2 * gq), 1) & (gq - 1)
        for j in range(SB_QGROUP):
            bias_scr[j] = jnp.where(key + j * bk < qry, 0.0, MASK_BIAS)

    qn = _pair_rms(q_ref[...].astype(F32), qg_ref[...], lo_lanes) * (SB_DH ** -0.5 * LOG2_E)
    q2t_scr[:, :gq] = jnp.where(lo_lanes, qn, 0.0).T.astype(BF16)
    q2t_scr[:, gq:] = jnp.where(lo_lanes, 0.0, qn).T.astype(BF16)
    acc_scr[...] = jnp.zeros_like(acc_scr)
    c_scr[...] = jnp.zeros_like(c_scr)

    last = grp * SB_QGROUP + SB_QGROUP - 1

    def stage_scores(p, masked):
        for u in range(SB_UNROLL):
            m = p * SB_UNROLL + u
            kt = kn_scr[pl.ds(pl.multiple_of((last - m) * bk, bk), bk), :]
            z = _dot(kt, q2t_scr[...])
            if masked:
                z = z + bias_scr[SB_QGROUP - 1 - m]
            sp = jnp.maximum(z, 0.0) + jnp.log2(1.0 + jnp.exp2(-jnp.abs(z)))
            sp_scr[u] = sp.astype(BF16)
            zs_scr[u] = z - sp
            sp0_scr[u] = sp[0:8]

    def stage_weights(p):
        c = c_scr[...]
        for u in range(SB_UNROLL):
            later = _dot(ut_ref[...], sp_scr[u])
            w_scr[u] = jnp.exp2(zs_scr[u] - later - c).astype(BF16)
            c = c + later[0:1] + sp0_scr[u][0:1]
        c_scr[...] = c

    def stage_values(p):
        for u in range(SB_UNROLL):
            vt = vt_scr[last - (p * SB_UNROLL + u)]
            w = w_scr[u]
            acc_scr[0:SB_DH, :] += _dot(vt[0:SB_DH], w[:, :gq])
            acc_scr[SB_DH:, :] += _dot(vt[SB_DH:], w[:, gq:])

    n_masked = SB_QGROUP // SB_UNROLL
    for p in range(n_masked):
        if p >= 2:
            stage_values(p - 2)
        if p >= 1:
            stage_weights(p - 1)
        stage_scores(p, True)

    n_trips = grp * (n_masked // SB_LOOP_STEPS)

    def keep_going(carry):
        trip, c_min = carry
        return jnp.logical_and(trip < n_trips, c_min < SB_DEAD_LOG2)

    def body(carry):
        trip, _ = carry
        for s in range(SB_LOOP_STEPS):
            p = n_masked + trip * SB_LOOP_STEPS + s
            stage_values(p - 2)
            stage_weights(p - 1)
            stage_scores(p, False)
        return trip + 1, jnp.min(c_scr[...])

    trips, _ = lax.while_loop(keep_going, body, (jnp.int32(0), jnp.float32(0.0)))
    p_end = n_masked + trips * SB_LOOP_STEPS
    stage_values(p_end - 2)
    stage_weights(p_end - 1)
    stage_values(p_end - 1)

    o_ref[...] = _pair_rms(acc_scr[...].T, og_ref[...], lo_lanes).astype(o_ref.dtype)


def _sb_attention(proj, qg, kg, og, *, batch):
    m = proj.shape[0]
    t = m // batch
    bk = SB_BLOCK
    gq = SB_QGROUP * bk
    n_groups = t // gq
    n_pairs = SB_HEADS * SB_DH // LANES
    oq, ok, ov = (_SEG[s][1] // LANES for s in ("sq", "sk", "sv"))
    j = np.arange(bk)
    ut = jnp.asarray((j[None, :] > j[:, None]).astype(np.float32), BF16)
    pair = lambda g: jnp.tile(g, 2).reshape(1, LANES)
    const = lambda shape: pl.BlockSpec(shape, lambda b, p, i: (0, 0))
    return pl.pallas_call(
        _sb_kernel,
        grid=(batch, n_pairs, n_groups),
        in_specs=[pl.BlockSpec((gq, LANES), lambda b, p, i: (b * n_groups + i, oq + p)),
                  pl.BlockSpec((t, LANES), lambda b, p, i: (b, ok + p)),
                  pl.BlockSpec((t, LANES), lambda b, p, i: (b, ov + p)),
                  const((1, LANES)), const((1, LANES)), const((1, LANES)),
                  const((bk, bk))],
        out_specs=pl.BlockSpec((gq, LANES), lambda b, p, i: (b * n_groups + i, p)),
        out_shape=jax.ShapeDtypeStruct((m, n_pairs * LANES), BF16),
        scratch_shapes=[pltpu.VMEM((t, LANES), BF16),
                        pltpu.VMEM((t // bk, LANES, bk), BF16),
                        pltpu.VMEM((LANES, 2 * gq), BF16),
                        pltpu.VMEM((1, 2 * gq), F32),
                        pltpu.VMEM((LANES, gq), F32),
                        pltpu.VMEM((SB_QGROUP, bk, 2 * gq), F32),
                        pltpu.VMEM((SB_UNROLL, bk, 2 * gq), BF16),
                        pltpu.VMEM((SB_UNROLL, bk, 2 * gq), F32),
                        pltpu.VMEM((SB_UNROLL, 8, 2 * gq), F32),
                        pltpu.VMEM((SB_UNROLL, bk, 2 * gq), BF16)],
        compiler_params=pltpu.CompilerParams(
            dimension_semantics=("arbitrary", "arbitrary", "arbitrary"),
            vmem_limit_bytes=VMEM_LIMIT),
        name="sb_attention",
    )(proj, proj, proj, pair(qg), pair(kg), pair(og), ut)


def kernel(x, norm_mix_g, w_in, gla_w_decay, gla_b_decay, gla_out_g, sb_q_g, sb_k_g, sb_out_g,
           hg_out_g, hg_lb_logits, w_out, norm_ffn_g, w_ffn_up, w_ffn_down):
    batch, seq, d_model = x.shape
    depth = w_in.shape[0]
    x2 = x.reshape(batch * seq, d_model).astype(F32)

    gla_cols = _padded_head_cols(0, GLA_HEADS, GLA_DK, GLA_DK_PAD)
    rows_a = _padded_head_cols(0, GLA_HEADS, GLA_DV, HEAD_V_PAD)
    sb_lo = GLA_HEADS * GLA_DV
    hg_lo = sb_lo + SB_HEADS * SB_DH
    rows_c = _padded_head_cols(hg_lo, HG_HEADS, HG_DV, HEAD_V_PAD)

    for li in range(depth):
        w_main = _take_padded(w_in[li], _MAIN_SRC, 1).astype(BF16)
        w_gate = _take_padded(w_in[li], _GATE_SRC, 1).astype(BF16)
        projs = _in_proj(x2, norm_mix_g[li], w_main, w_gate, tm=1024, tn=MAIN_COLS // 3)

        wd = _take_padded(gla_w_decay[li].astype(F32), gla_cols, 1)
        wd = jnp.pad(wd, ((0, LANES - GLA_LOWRANK), (0, 0)))
        bd = _take_padded(gla_b_decay[li].astype(F32), gla_cols, 0).reshape(1, -1)
        gain_a = jnp.pad(gla_out_g[li].astype(F32), (0, HEAD_V_PAD - GLA_DV)).reshape(1, -1)
        o_a = _mixer_call(_gla_kernel, projs, ("gq", "gk", "glr"), (wd, bd),
                          ("gv", "gg"), (gain_a,), batch=batch, heads=GLA_HEADS,
                          dk_lanes=GLA_DK_PAD, name="gla_mixer")

        o_b = _sb_attention(projs[0], sb_q_g[li].astype(F32), sb_k_g[li].astype(F32),
                            sb_out_g[li].astype(F32), batch=batch)

        gain_c = jnp.pad(hg_out_g[li].astype(F32), (0, HEAD_V_PAD - HG_DV)).reshape(1, -1)
        o_c = _mixer_call(functools.partial(_hg_kernel, li), projs, ("hq", "hf"),
                          (hg_lb_logits.astype(F32),), ("hi", "hg"), (gain_c,), batch=batch,
                          heads=HG_HEADS, dk_lanes=HG_DK, name="hg_mixer")

        wo = w_out[li]
        w_a = _take_padded(wo, rows_a, 0).astype(BF16)
        w_b = wo[sb_lo:hg_lo].astype(BF16)
        w_c = _take_padded(wo, rows_c, 0).astype(BF16)
        x2 = _residual_matmul(x2, (o_a, o_b, o_c), (w_a, w_b, w_c), tm=1024)

        act = _ffn_up(x2, norm_ffn_g[li], w_ffn_up[li].astype(BF16), tm=1024, tn=1408)
        x2 = _residual_matmul(x2, (act,), (w_ffn_down[li].astype(BF16),), tm=512)
    return x2.reshape(batch, seq, d_model).astype(x.dtype)
```

```python
import functools

import numpy as np
import jax
import jax.numpy as jnp
from jax import lax
from jax.experimental import pallas as pl
from jax.experimental.pallas import tpu as pltpu

F32 = jnp.float32
BF16 = jnp.bfloat16

LANES = 128
RMS_EPS = 1e-6

GLA_HEADS, GLA_DK, GLA_DV, GLA_LOWRANK = 4, 48, 96, 16
GLA_GATE_NORMALIZER = 16.0
SB_HEADS, SB_DH, SB_BLOCK = 6, 64, 128
HG_HEADS, HG_DK, HG_DV = 4, 128, 64

GLA_DK_PAD = 64
HEAD_V_PAD = LANES
MIX_CHUNK = 128
SB_QGROUP = 4
SB_UNROLL = 2
SB_LOOP_STEPS = 1
SB_DEAD_LOG2 = 150.0
LOG2_E = 1.4426950408889634
MASK_BIAS = -1e30
assert SB_QGROUP % SB_UNROLL == 0 and SB_QGROUP // SB_UNROLL >= 2
assert (SB_QGROUP // SB_UNROLL) % SB_LOOP_STEPS == 0
VMEM_LIMIT = 48 * 1024 * 1024

_MAIN_SEGS = (("gv", 512), ("gg", 512), ("hq", 512), ("hi", 512), ("hg", 512),
              ("gq", 256), ("gk", 256), ("sq", 384), ("sk", 384), ("sv", 384))
_GATE_SEGS = (("hf", 512), ("glr", 128))
_SEG = {}
for _arr, _segs in enumerate((_MAIN_SEGS, _GATE_SEGS)):
    _off = 0
    for _name, _w in _segs:
        _SEG[_name] = (_arr, _off, _w)
        _off += _w
MAIN_COLS = sum(w for _, w in _MAIN_SEGS)
GATE_COLS = sum(w for _, w in _GATE_SEGS)


def _padded_head_cols(start, heads, width, pad):
    idx = -np.ones((heads, pad), np.int64)
    idx[:, :width] = start + np.arange(heads)[:, None] * width + np.arange(width)[None, :]
    return idx.reshape(-1)


def _proj_source_columns():
    sizes = (GLA_HEADS * GLA_DK, GLA_HEADS * GLA_DK, GLA_HEADS * GLA_DV, GLA_LOWRANK,
             GLA_HEADS * GLA_DV, SB_HEADS * SB_DH, SB_HEADS * SB_DH, SB_HEADS * SB_DH,
             HG_HEADS * HG_DK, HG_HEADS * HG_DK, HG_HEADS * HG_DV, HG_HEADS * HG_DV)
    starts = np.concatenate([[0], np.cumsum(sizes)[:-1]])
    (gq, gk, gv, glr, gg, sq, sk, sv, hq, hf, hi, hg) = [int(s) for s in starts]
    src = {
        "gq": _padded_head_cols(gq, GLA_HEADS, GLA_DK, GLA_DK_PAD),
        "gk": _padded_head_cols(gk, GLA_HEADS, GLA_DK, GLA_DK_PAD),
        "gv": _padded_head_cols(gv, GLA_HEADS, GLA_DV, HEAD_V_PAD),
        "gg": _padded_head_cols(gg, GLA_HEADS, GLA_DV, HEAD_V_PAD),
        "glr": _padded_head_cols(glr, 1, GLA_LOWRANK, LANES),
        "sq": np.arange(sq, sq + SB_HEADS * SB_DH),
        "sk": np.arange(sk, sk + SB_HEADS * SB_DH),
        "sv": np.arange(sv, sv + SB_HEADS * SB_DH),
        "hq": np.arange(hq, hq + HG_HEADS * HG_DK),
        "hf": np.arange(hf, hf + HG_HEADS * HG_DK),
        "hi": _padded_head_cols(hi, HG_HEADS, HG_DV, HEAD_V_PAD),
        "hg": _padded_head_cols(hg, HG_HEADS, HG_DV, HEAD_V_PAD),
    }
    return tuple(np.concatenate([src[name] for name, _ in segs]) for segs in (_MAIN_SEGS, _GATE_SEGS))


_MAIN_SRC, _GATE_SRC = _proj_source_columns()


def _take_padded(arr, src, axis):
    pad = src < 0
    breaks = np.flatnonzero(np.where(pad[1:] | pad[:-1], pad[1:] != pad[:-1], np.diff(src) != 1)) + 1
    pieces = []
    for run in np.split(src, breaks):
        if run[0] < 0:
            shape = list(arr.shape)
            shape[axis] = len(run)
            pieces.append(jnp.zeros(shape, arr.dtype))
        else:
            pieces.append(lax.slice_in_dim(arr, int(run[0]), int(run[-1]) + 1, axis=axis))
    return jnp.concatenate(pieces, axis=axis)


def _dot(a, b):
    return jnp.dot(a, b, preferred_element_type=F32)


def _dot_nt(a, b):
    return lax.dot_general(a, b, (((1,), (1,)), ((), ())), preferred_element_type=F32)


def _split_bf16(x):
    hi = x.astype(BF16)
    lo = (x - hi.astype(F32)).astype(BF16)
    return hi, lo


def _softplus(z):
    return jnp.maximum(z, 0.0) + jnp.log(1.0 + jnp.exp(-jnp.abs(z)))


def _silu(z):
    return z / (1.0 + jnp.exp(-z))


def _rms_rows(x, g):
    ms = jnp.mean(x * x, axis=-1, keepdims=True)
    return x * lax.rsqrt(ms + RMS_EPS) * g


def _in_proj_kernel(x_ref, g_ref, w_ref, wg_ref, o_ref, og_ref, h_scr):
    @pl.when(pl.program_id(1) == 0)
    def _():
        h = _rms_rows(x_ref[...], g_ref[...]).astype(BF16)
        h_scr[...] = h
        og_ref[...] = _dot(h, wg_ref[...])

    o_ref[...] = _dot(h_scr[...], w_ref[...]).astype(o_ref.dtype)


def _in_proj(x, g, w_main, w_gate, *, tm, tn):
    m, d = x.shape
    n, ng = w_main.shape[1], w_gate.shape[1]
    return pl.pallas_call(
        _in_proj_kernel,
        grid=(m // tm, n // tn),
        in_specs=[pl.BlockSpec((tm, d), lambda i, j: (i, 0)),
                  pl.BlockSpec((1, d), lambda i, j: (0, 0)),
                  pl.BlockSpec((d, tn), lambda i, j: (0, j)),
                  pl.BlockSpec((d, ng), lambda i, j: (0, 0))],
        out_specs=[pl.BlockSpec((tm, tn), lambda i, j: (i, j)),
                   pl.BlockSpec((tm, ng), lambda i, j: (i, 0))],
        out_shape=[jax.ShapeDtypeStruct((m, n), BF16), jax.ShapeDtypeStruct((m, ng), F32)],
        scratch_shapes=[pltpu.VMEM((tm, d), BF16)],
        compiler_params=pltpu.CompilerParams(
            dimension_semantics=("arbitrary", "arbitrary"), vmem_limit_bytes=VMEM_LIMIT),
        name="in_proj",
    )(x, g.reshape(1, d), w_main, w_gate)


def _ffn_up_kernel(x_ref, g_ref, wg_ref, wu_ref, o_ref, h_scr):
    @pl.when(pl.program_id(1) == 0)
    def _():
        h_scr[...] = _rms_rows(x_ref[...], g_ref[...]).astype(BF16)

    h = h_scr[...]
    gate = _dot(h, wg_ref[...])
    up = _dot(h, wu_ref[...])
    o_ref[...] = (_silu(gate) * up).astype(o_ref.dtype)


def _ffn_up(x, g, w_up, *, tm, tn):
    m, d = x.shape
    d_ff = w_up.shape[1] // 2
    nj = d_ff // tn
    return pl.pallas_call(
        _ffn_up_kernel,
        grid=(m // tm, nj),
        in_specs=[pl.BlockSpec((tm, d), lambda i, j: (i, 0)),
                  pl.BlockSpec((1, d), lambda i, j: (0, 0)),
                  pl.BlockSpec((d, tn), lambda i, j: (0, j)),
                  pl.BlockSpec((d, tn), lambda i, j: (0, j + nj))],
        out_specs=pl.BlockSpec((tm, tn), lambda i, j: (i, j)),
        out_shape=jax.ShapeDtypeStruct((m, d_ff), BF16),
        scratch_shapes=[pltpu.VMEM((tm, d), BF16)],
        compiler_params=pltpu.CompilerParams(
            dimension_semantics=("arbitrary", "arbitrary"), vmem_limit_bytes=VMEM_LIMIT),
        name="ffn_up",
    )(x, g.reshape(1, d), w_up, w_up)


def _residual_matmul_kernel(n_in, res_ref, *refs):
    a_refs, w_refs, o_ref = refs[:n_in], refs[n_in:2 * n_in], refs[2 * n_in]
    acc = res_ref[...]
    for a_ref, w_ref in zip(a_refs, w_refs):
        acc = acc + _dot(a_ref[...], w_ref[...])
    o_ref[...] = acc


def _residual_matmul(res, acts, weights, *, tm):
    m, n = res.shape
    n_in = len(acts)
    in_specs = [pl.BlockSpec((tm, n), lambda i: (i, 0))]
    in_specs += [pl.BlockSpec((tm, a.shape[1]), lambda i: (i, 0)) for a in acts]
    in_specs += [pl.BlockSpec(w.shape, lambda i: (0, 0)) for w in weights]
    return pl.pallas_call(
        functools.partial(_residual_matmul_kernel, n_in),
        grid=(m // tm,),
        in_specs=in_specs,
        out_specs=pl.BlockSpec((tm, n), lambda i: (i, 0)),
        out_shape=jax.ShapeDtypeStruct((m, n), F32),
        compiler_params=pltpu.CompilerParams(
            dimension_semantics=("arbitrary",), vmem_limit_bytes=VMEM_LIMIT),
        name="residual_matmul",
    )(res, *acts, *weights)


def _decay_sum_matrix(chunk):
    t = np.arange(chunk)
    blocks = [(t[None, :] <= t[:, None]), (t[None, :] > t[:, None])]
    h = 1
    while h < chunk:
        mid = (t // (2 * h)) * (2 * h) + h
        right = t >= mid
        m = np.where(right[:, None],
                     (t[None, :] >= mid[:, None]) & (t[None, :] <= t[:, None]),
                     (t[None, :] > t[:, None]) & (t[None, :] < mid[:, None]))
        blocks.append(m)
        h *= 2
    return np.concatenate(blocks, axis=0).astype(np.float32)


def _pair_owner_matrix(chunk):
    t = np.arange(chunk)
    x = t[:, None] ^ t[None, :]
    level = np.floor(np.log2(np.maximum(x, 1))).astype(np.int32)
    return np.where(x == 0, 0, np.where(t[None, :] < t[:, None], 1 + level, -1)).astype(np.int32)


def _chunk_step(qb, kb, lf, v_ref, gate_ref, gain, first_chunk, msum_ref, owner_ref,
                rd, wr, state_ref, o_ref, *, heads, dk_lanes, dv):
    q_rd, k_rd, dec_rd = rd
    q_wr, k_wr, dec_wr = wr
    c = qb.shape[0]
    n_levels = c.bit_length() - 1
    n_tiles = n_levels + 1
    lf_b = lf.astype(BF16)

    def decay_rows(block):
        return jnp.exp(_dot(msum_ref[block * c:(block + 1) * c, :], lf_b).astype(BF16))

    def store_slot(t):
        if t == 0:
            q_wr[0] = qb
            k_wr[0] = kb.T
        elif t <= n_levels:
            e = decay_rows(1 + t)
            q_wr[t] = qb * e
            k_wr[t] = (kb * e).T
        else:
            q_wr[t] = qb * decay_rows(0)
            k_wr[t] = (kb * decay_rows(1)).T
            total = lax.dot_general(lf_b, jnp.ones((c, LANES), BF16), (((0,), (0,)), ((), ())),
                                    preferred_element_type=F32)
            dec_wr[...] = jnp.exp(total)

    owner = owner_ref[...]
    owned = [owner == t for t in range(n_tiles)]
    lane = lax.broadcasted_iota(jnp.int32, (1, LANES), 1)

    for hd in range(heads):
        g0 = (hd * dk_lanes // LANES) * LANES
        grp = slice(g0, g0 + LANES)
        if dk_lanes < LANES:
            lo = hd * dk_lanes - g0
            head_lanes = jnp.where((lane >= lo) & (lane < lo + dk_lanes), 1.0, 0.0).astype(BF16)
            pick = lambda t: q_rd[t, :, grp] * head_lanes
        else:
            pick = lambda t: q_rd[t, :, grp]
        scores = jnp.zeros((c, c), F32)
        for t in range(n_tiles):
            scores = jnp.where(owned[t], _dot(pick(t), k_rd[t, grp, :]), scores)
        sl = slice(hd * LANES, (hd + 1) * LANES)
        v_h = v_ref[:, sl]
        state = jnp.where(first_chunk, 0.0, state_ref[hd])
        o = _dot(scores.astype(BF16), v_h) + _dot(pick(n_tiles), state.astype(BF16))
        state_ref[hd] = state * dec_rd[grp, :] + _dot(k_rd[n_tiles, grp, :], v_h)
        ms = jnp.sum(o * o, axis=-1, keepdims=True) * (1.0 / dv)
        y = o * lax.rsqrt(ms + RMS_EPS) * gain
        o_ref[:, sl] = (y * _silu(gate_ref[:, sl].astype(F32))).astype(o_ref.dtype)

        for t in range(hd, n_tiles + 1, heads):
            store_slot(t)
        yield


def _mixer_step(scratch, step_fn):
    @pl.when(pl.program_id(0) == 0)
    def _():
        for r in scratch:
            r[...] = jnp.zeros_like(r)

    parity = lax.rem(pl.program_id(0), 2)
    for buf in (0, 1):
        @pl.when(parity == buf)
        def _():
            step_fn(1 - buf, buf)


def _mixers_kernel(layer, n_chunks,
                   gq_ref, gk_ref, lr_ref, wd_ref, bd_ref, hq_ref, hf_ref, lb_ref,
                   gv_ref, gg_ref, gain_a_ref, hi_ref, hg_ref, gain_c_ref, msum_ref, owner_ref,
                   oa_ref, oc_ref,
                   state_a, q_a, k_a, dec_a, state_c, q_c, k_c, dec_c):
    first_chunk = lax.rem(pl.program_id(0) + n_chunks - 1, n_chunks) == 0

    def step(rd, wr):
        lr_hi, lr_lo = _split_bf16(lr_ref[...])
        wd_hi, wd_lo = _split_bf16(wd_ref[...])
        logits = _dot(lr_hi, wd_hi) + _dot(lr_hi, wd_lo) + _dot(lr_lo, wd_hi) + bd_ref[...]
        lf_a = -_softplus(-logits) * (1.0 / GLA_GATE_NORMALIZER)
        qb_a = (gq_ref[...].astype(F32) * GLA_DK ** -0.5).astype(BF16)

        lb_logits = lb_ref[...]
        ex = jnp.exp(lb_logits - jnp.max(lb_logits, axis=0, keepdims=True))
        probs = ex / jnp.sum(ex, axis=0, keepdims=True)
        lb = jnp.zeros_like(probs[0:1])
        for d in range(1, layer + 1):
            lb = lb + probs[d:d + 1]
        hf = hf_ref[...]
        sp = _softplus(-hf)
        log_sig = -sp
        a = jnp.log(jnp.maximum(lb, 1e-30))
        b = jnp.log(1.0 - lb) + log_sig
        lae = jnp.maximum(a, b) + jnp.log(1.0 + jnp.exp(-jnp.abs(a - b)))
        lf_c = jnp.where(lb > 0.0, lae, log_sig)
        kb_c = ((1.0 - lb) * jnp.exp(-(hf + sp))).astype(BF16)

        gla = _chunk_step(qb_a, gk_ref[...], lf_a, gv_ref, gg_ref, gain_a_ref[...], first_chunk,
                          msum_ref, owner_ref, (q_a.at[rd], k_a.at[rd], dec_a.at[rd]),
                          (q_a.at[wr], k_a.at[wr], dec_a.at[wr]), state_a, oa_ref,
                          heads=GLA_HEADS, dk_lanes=GLA_DK_PAD, dv=GLA_DV)
        hgrn = _chunk_step(hq_ref[...], kb_c, lf_c, hi_ref, hg_ref, gain_c_ref[...], first_chunk,
                           msum_ref, owner_ref, (q_c.at[rd], k_c.at[rd], dec_c.at[rd]),
                           (q_c.at[wr], k_c.at[wr], dec_c.at[wr]), state_c, oc_ref,
                           heads=HG_HEADS, dk_lanes=HG_DK, dv=HG_DV)
        for _ in zip(gla, hgrn):
            pass

    _mixer_step((state_a, q_a, k_a, dec_a, state_c, q_c, k_c, dec_c), step)


def _mixers(layer, projs, wd, bd, gain_a, lb_logits, gain_c, *, batch):
    assert GLA_HEADS == HG_HEADS
    m = projs[0].shape[0]
    n_chunks = m // batch // MIX_CHUNK
    n_steps = m // MIX_CHUNK
    msum = jnp.asarray(_decay_sum_matrix(MIX_CHUNK), BF16)
    owner = jnp.asarray(_pair_owner_matrix(MIX_CHUNK))
    lead = lambda s: jnp.minimum(s, n_steps - 1)
    lag = lambda s: jnp.maximum(s - 1, 0)

    def seg(name, row_of):
        arr, off, width = _SEG[name]
        assert off % width == 0
        return projs[arr], pl.BlockSpec((MIX_CHUNK, width), lambda s: (row_of(s), off // width))

    const = lambda e: (e, pl.BlockSpec(e.shape, lambda s: (0, 0)))
    operands = [seg("gq", lead), seg("gk", lead), seg("glr", lead), const(wd), const(bd),
                seg("hq", lead), seg("hf", lead), const(lb_logits),
                seg("gv", lag), seg("gg", lag), const(gain_a),
                seg("hi", lag), seg("hg", lag), const(gain_c), const(msum), const(owner)]
    width = GLA_HEADS * HEAD_V_PAD
    n_slots = MIX_CHUNK.bit_length() + 1

    def scratch(dk_lanes):
        w = GLA_HEADS * dk_lanes
        return [pltpu.VMEM((GLA_HEADS, LANES, HEAD_V_PAD), F32),
                pltpu.VMEM((2, n_slots, MIX_CHUNK, w), BF16),
                pltpu.VMEM((2, n_slots, w, MIX_CHUNK), BF16),
                pltpu.VMEM((2, w, LANES), F32)]

    out_spec = pl.BlockSpec((MIX_CHUNK, width), lambda s: (lag(s), 0))
    return pl.pallas_call(
        functools.partial(_mixers_kernel, layer, n_chunks),
        grid=(n_steps + 1,),
        in_specs=[spec for _, spec in operands],
        out_specs=[out_spec, out_spec],
        out_shape=[jax.ShapeDtypeStruct((m, width), BF16)] * 2,
        scratch_shapes=scratch(GLA_DK_PAD) + scratch(HG_DK),
        compiler_params=pltpu.CompilerParams(
            dimension_semantics=("arbitrary",), vmem_limit_bytes=VMEM_LIMIT),
        name="mixers",
    )(*[arr for arr, _ in operands])


def _pair_rms(x, g, lo_lanes):
    x2 = x * x
    s_lo = jnp.sum(jnp.where(lo_lanes, x2, 0.0), axis=-1, keepdims=True)
    s_hi = jnp.sum(jnp.where(lo_lanes, 0.0, x2), axis=-1, keepdims=True)
    ms = jnp.where(lo_lanes, s_lo, s_hi) * (1.0 / SB_DH)
    return x * lax.rsqrt(ms + RMS_EPS) * g


def _sb_kernel(q_ref, k_ref, v_ref, qg_ref, kg_ref, og_ref, ut_ref, o_ref,
               kn_scr, vt_scr, q2t_scr, c_scr, acc_scr,
               bias_scr, sp_scr, zs_scr, sp0_scr, w_scr):
    grp = pl.program_id(2)
    bk = SB_BLOCK
    gq = q_ref.shape[0]
    n_kb = k_ref.shape[0] // bk
    lo_lanes = lax.broadcasted_iota(jnp.int32, (1, LANES), 1) < SB_DH

    @pl.when(grp == 0)
    def _():
        kn_scr[...] = _pair_rms(k_ref[...].astype(F32), kg_ref[...], lo_lanes).astype(BF16)

        def transpose_block(kb, carry):
            rows = pl.ds(pl.multiple_of(kb * bk, bk), bk)
            vt_scr[kb] = v_ref[rows, :].astype(F32).T.astype(BF16)
            return carry

        lax.fori_loop(0, n_kb, transpose_block, 0)

        key = lax.broadcasted_iota(jnp.int32, (bk, bk), 0)
        qry = lax.broadcasted_iota(jnp.int32, (bk, bk), 1)
        bias_scr[...] = jnp.where(key < qry, 0.0, MASK_BIAS)

    qn = _pair_rms(q_ref[...].astype(F32), qg_ref[...], lo_lanes) * (SB_DH ** -0.5 * LOG2_E)
    q2t_scr[:, :gq] = jnp.where(lo_lanes, qn, 0.0).T.astype(BF16)
    q2t_scr[:, gq:] = jnp.where(lo_lanes, 0.0, qn).T.astype(BF16)
    acc_scr[...] = jnp.zeros_like(acc_scr)
    c_scr[...] = jnp.zeros_like(c_scr)

    last = grp * SB_QGROUP + SB_QGROUP - 1

    def score_columns(kt, u, cols, diagonal):
        z = _dot(kt, q2t_scr[:, cols])
        if diagonal:
            parts = [z[:, :bk] + bias_scr[...]] + ([z[:, bk:]] if z.shape[1] > bk else [])
            z = jnp.concatenate(parts, axis=1)
        sp = jnp.maximum(z, 0.0) + jnp.log2(1.0 + jnp.exp2(-jnp.abs(z)))
        sp_scr[u, :, cols] = sp.astype(BF16)
        zs_scr[u, :, cols] = z - sp
        sp0_scr[u, :, cols] = sp[0:8]

    def stage_scores(p, masked):
        for u in range(SB_UNROLL):
            m = p * SB_UNROLL + u
            kt = kn_scr[pl.ds(pl.multiple_of((last - m) * bk, bk), bk), :]
            if not masked:
                score_columns(kt, u, slice(0, 2 * gq), False)
                continue
            lo = (SB_QGROUP - 1 - m) * bk
            for half in (0, gq):
                score_columns(kt, u, slice(half + lo, half + gq), True)
                if lo:
                    hidden = slice(half, half + lo)
                    sp_scr[u, :, hidden] = jnp.zeros((bk, lo), BF16)
                    zs_scr[u, :, hidden] = jnp.full((bk, lo), MASK_BIAS, F32)
                    sp0_scr[u, :, hidden] = jnp.zeros((8, lo), F32)

    def stage_weights(p):
        c = c_scr[...]
        for u in range(SB_UNROLL):
            later = _dot(ut_ref[...], sp_scr[u])
            w_scr[u] = jnp.exp2(zs_scr[u] - later - c).astype(BF16)
            c = c + later[0:1] + sp0_scr[u, 0:1, :]
        c_scr[...] = c

    def stage_values(p):
        for u in range(SB_UNROLL):
            vt = vt_scr[last - (p * SB_UNROLL + u)]
            w = w_scr[u]
            acc_scr[0:SB_DH, :] += _dot(vt[0:SB_DH], w[:, :gq])
            acc_scr[SB_DH:, :] += _dot(vt[SB_DH:], w[:, gq:])

    n_masked = SB_QGROUP // SB_UNROLL
    for p in range(n_masked):
        if p >= 2:
            stage_values(p - 2)
        if p >= 1:
            stage_weights(p - 1)
        stage_scores(p, True)

    n_trips = grp * (n_masked // SB_LOOP_STEPS)

    def keep_going(carry):
        trip, c_min = carry
        return jnp.logical_and(trip < n_trips, c_min < SB_DEAD_LOG2)

    def body(carry):
        trip, _ = carry
        for s in range(SB_LOOP_STEPS):
            p = n_masked + trip * SB_LOOP_STEPS + s
            stage_values(p - 2)
            stage_weights(p - 1)
            stage_scores(p, False)
        return trip + 1, jnp.min(c_scr[...])

    trips, c_min = lax.while_loop(keep_going, body, (jnp.int32(0), jnp.float32(0.0)))
    p_end = n_masked + trips * SB_LOOP_STEPS
    stage_values(p_end - 2)

    @pl.when(c_min < SB_DEAD_LOG2)
    def _():
        stage_weights(p_end - 1)
        stage_values(p_end - 1)

    o_ref[...] = _pair_rms(acc_scr[...].T, og_ref[...], lo_lanes).astype(o_ref.dtype)


def _sb_attention(proj, qg, kg, og, *, batch):
    m = proj.shape[0]
    t = m // batch
    bk = SB_BLOCK
    gq = SB_QGROUP * bk
    n_groups = t // gq
    n_pairs = SB_HEADS * SB_DH // LANES
    oq, ok, ov = (_SEG[s][1] // LANES for s in ("sq", "sk", "sv"))
    j = np.arange(bk)
    ut = jnp.asarray((j[None, :] > j[:, None]).astype(np.float32), BF16)
    pair = lambda g: jnp.tile(g, 2).reshape(1, LANES)
    const = lambda shape: pl.BlockSpec(shape, lambda b, p, i: (0, 0))
    return pl.pallas_call(
        _sb_kernel,
        grid=(batch, n_pairs, n_groups),
        in_specs=[pl.BlockSpec((gq, LANES), lambda b, p, i: (b * n_groups + i, oq + p)),
                  pl.BlockSpec((t, LANES), lambda b, p, i: (b, ok + p)),
                  pl.BlockSpec((t, LANES), lambda b, p, i: (b, ov + p)),
                  const((1, LANES)), const((1, LANES)), const((1, LANES)),
                  const((bk, bk))],
        out_specs=pl.BlockSpec((gq, LANES), lambda b, p, i: (b * n_groups + i, p)),
        out_shape=jax.ShapeDtypeStruct((m, n_pairs * LANES), BF16),
        scratch_shapes=[pltpu.VMEM((t, LANES), BF16),
                        pltpu.VMEM((t // bk, LANES, bk), BF16),
                        pltpu.VMEM((LANES, 2 * gq), BF16),
                        pltpu.VMEM((1, 2 * gq), F32),
                        pltpu.VMEM((LANES, gq), F32),
                        pltpu.VMEM((bk, bk), F32),
                        pltpu.VMEM((SB_UNROLL, bk, 2 * gq), BF16),
                        pltpu.VMEM((SB_UNROLL, bk, 2 * gq), F32),
                        pltpu.VMEM((SB_UNROLL, 8, 2 * gq), F32),
                        pltpu.VMEM((SB_UNROLL, bk, 2 * gq), BF16)],
        compiler_params=pltpu.CompilerParams(
            dimension_semantics=("arbitrary", "arbitrary", "arbitrary"),
            vmem_limit_bytes=VMEM_LIMIT),
        name="sb_attention",
    )(proj, proj, proj, pair(qg), pair(kg), pair(og), ut)


def kernel(x, norm_mix_g, w_in, gla_w_decay, gla_b_decay, gla_out_g, sb_q_g, sb_k_g, sb_out_g,
           hg_out_g, hg_lb_logits, w_out, norm_ffn_g, w_ffn_up, w_ffn_down):
    batch, seq, d_model = x.shape
    depth = w_in.shape[0]
    x2 = x.reshape(batch * seq, d_model).astype(F32)

    gla_cols = _padded_head_cols(0, GLA_HEADS, GLA_DK, GLA_DK_PAD)
    sb_lo = GLA_HEADS * GLA_DV
    hg_lo = sb_lo + SB_HEADS * SB_DH
    w_main = _take_padded(w_in, _MAIN_SRC, 2).astype(BF16)
    w_gate = _take_padded(w_in, _GATE_SRC, 2).astype(BF16)
    wd = _take_padded(gla_w_decay.astype(F32), gla_cols, 2)
    wd = jnp.pad(wd, ((0, 0), (0, LANES - GLA_LOWRANK), (0, 0)))
    bd = _take_padded(gla_b_decay.astype(F32), gla_cols, 1)[:, None, :]
    gain_a = jnp.pad(gla_out_g.astype(F32), ((0, 0), (0, HEAD_V_PAD - GLA_DV)))[:, None, :]
    gain_c = jnp.pad(hg_out_g.astype(F32), ((0, 0), (0, HEAD_V_PAD - HG_DV)))[:, None, :]
    w_a = _take_padded(w_out, _padded_head_cols(0, GLA_HEADS, GLA_DV, HEAD_V_PAD), 1).astype(BF16)
    w_b = w_out[:, sb_lo:hg_lo].astype(BF16)
    w_c = _take_padded(w_out, _padded_head_cols(hg_lo, HG_HEADS, HG_DV, HEAD_V_PAD), 1).astype(BF16)
    w_up = w_ffn_up.astype(BF16)
    w_down = w_ffn_down.astype(BF16)
    lb_logits = hg_lb_logits.astype(F32)

    for li in range(depth):
        projs = _in_proj(x2, norm_mix_g[li], w_main[li], w_gate[li], tm=1024, tn=MAIN_COLS // 3)
        o_a, o_c = _mixers(li, projs, wd[li], bd[li], gain_a[li], lb_logits, gain_c[li],
                           batch=batch)
        o_b = _sb_attention(projs[0], sb_q_g[li].astype(F32), sb_k_g[li].astype(F32),
                            sb_out_g[li].astype(F32), batch=batch)
        x2 = _residual_matmul(x2, (o_a, o_b, o_c), (w_a[li], w_b[li], w_c[li]), tm=1024)
        act = _ffn_up(x2, norm_ffn_g[li], w_up[li], tm=1024, tn=1408)
        x2 = _residual_matmul(x2, (act,), (w_down[li],), tm=512)
    return x2.reshape(batch, seq, d_model).astype(x.dtype)
```

```python
import functools

import numpy as np
import jax
import jax.numpy as jnp
from jax import lax
from jax.experimental import pallas as pl
from jax.experimental.pallas import tpu as pltpu

F32 = jnp.float32
BF16 = jnp.bfloat16

LANES = 128
RMS_EPS = 1e-6

GLA_HEADS, GLA_DK, GLA_DV, GLA_LOWRANK = 4, 48, 96, 16
GLA_GATE_NORMALIZER = 16.0
SB_HEADS, SB_DH, SB_BLOCK = 6, 64, 128
HG_HEADS, HG_DK, HG_DV = 4, 128, 64

GLA_DK_PAD = 64
HEAD_V_PAD = LANES
MIX_CHUNK = 128
SB_QGROUP = 4
SB_UNROLL = 2
SB_LOOP_STEPS = 1
SB_DEAD_LOG2 = 150.0
LOG2_E = 1.4426950408889634
MASK_BIAS = -1e30
assert SB_QGROUP % SB_UNROLL == 0 and SB_QGROUP // SB_UNROLL >= 2
assert (SB_QGROUP // SB_UNROLL) % SB_LOOP_STEPS == 0
VMEM_LIMIT = 48 * 1024 * 1024

_MAIN_SEGS = (("gv", 512), ("gg", 512), ("hq", 512), ("hi", 256), ("hg", 256),
              ("gq", 256), ("gk", 256), ("sq", 384), ("sk", 384), ("sv", 384), ("pad", 128))
_GATE_SEGS = (("hf", 512), ("glr", 128))
_SEG = {}
for _arr, _segs in enumerate((_MAIN_SEGS, _GATE_SEGS)):
    _off = 0
    for _name, _w in _segs:
        _SEG[_name] = (_arr, _off, _w)
        _off += _w
MAIN_COLS = sum(w for _, w in _MAIN_SEGS)
GATE_COLS = sum(w for _, w in _GATE_SEGS)


def _padded_head_cols(start, heads, width, pad):
    idx = -np.ones((heads, pad), np.int64)
    idx[:, :width] = start + np.arange(heads)[:, None] * width + np.arange(width)[None, :]
    return idx.reshape(-1)


def _proj_source_columns():
    sizes = (GLA_HEADS * GLA_DK, GLA_HEADS * GLA_DK, GLA_HEADS * GLA_DV, GLA_LOWRANK,
             GLA_HEADS * GLA_DV, SB_HEADS * SB_DH, SB_HEADS * SB_DH, SB_HEADS * SB_DH,
             HG_HEADS * HG_DK, HG_HEADS * HG_DK, HG_HEADS * HG_DV, HG_HEADS * HG_DV)
    starts = np.concatenate([[0], np.cumsum(sizes)[:-1]])
    (gq, gk, gv, glr, gg, sq, sk, sv, hq, hf, hi, hg) = [int(s) for s in starts]
    src = {
        "gq": _padded_head_cols(gq, GLA_HEADS, GLA_DK, GLA_DK_PAD),
        "gk": _padded_head_cols(gk, GLA_HEADS, GLA_DK, GLA_DK_PAD),
        "gv": _padded_head_cols(gv, GLA_HEADS, GLA_DV, HEAD_V_PAD),
        "gg": _padded_head_cols(gg, GLA_HEADS, GLA_DV, HEAD_V_PAD),
        "glr": _padded_head_cols(glr, 1, GLA_LOWRANK, LANES),
        "sq": np.arange(sq, sq + SB_HEADS * SB_DH),
        "sk": np.arange(sk, sk + SB_HEADS * SB_DH),
        "sv": np.arange(sv, sv + SB_HEADS * SB_DH),
        "hq": np.arange(hq, hq + HG_HEADS * HG_DK),
        "hf": np.arange(hf, hf + HG_HEADS * HG_DK),
        "hi": np.arange(hi, hi + HG_HEADS * HG_DV),
        "hg": np.arange(hg, hg + HG_HEADS * HG_DV),
        "pad": -np.ones(LANES, np.int64),
    }
    return tuple(np.concatenate([src[name] for name, _ in segs]) for segs in (_MAIN_SEGS, _GATE_SEGS))


_MAIN_SRC, _GATE_SRC = _proj_source_columns()


def _take_padded(arr, src, axis):
    pad = src < 0
    breaks = np.flatnonzero(np.where(pad[1:] | pad[:-1], pad[1:] != pad[:-1], np.diff(src) != 1)) + 1
    pieces = []
    for run in np.split(src, breaks):
        if run[0] < 0:
            shape = list(arr.shape)
            shape[axis] = len(run)
            pieces.append(jnp.zeros(shape, arr.dtype))
        else:
            pieces.append(lax.slice_in_dim(arr, int(run[0]), int(run[-1]) + 1, axis=axis))
    return jnp.concatenate(pieces, axis=axis)


def _dot(a, b):
    return jnp.dot(a, b, preferred_element_type=F32)


def _dot_nt(a, b):
    return lax.dot_general(a, b, (((1,), (1,)), ((), ())), preferred_element_type=F32)


def _split_bf16(x):
    hi = x.astype(BF16)
    lo = (x - hi.astype(F32)).astype(BF16)
    return hi, lo


def _softplus(z):
    return jnp.maximum(z, 0.0) + jnp.log(1.0 + jnp.exp(-jnp.abs(z)))


def _silu(z):
    return z / (1.0 + jnp.exp(-z))


def _rms_rows(x, g):
    ms = jnp.mean(x * x, axis=-1, keepdims=True)
    return x * lax.rsqrt(ms + RMS_EPS) * g


def _in_proj_kernel(x_ref, g_ref, w_ref, wg_ref, o_ref, og_ref, h_scr):
    @pl.when(pl.program_id(1) == 0)
    def _():
        h = _rms_rows(x_ref[...], g_ref[...]).astype(BF16)
        h_scr[...] = h
        og_ref[...] = _dot(h, wg_ref[...])

    o_ref[...] = _dot(h_scr[...], w_ref[...]).astype(o_ref.dtype)


def _in_proj(layer, x, g, w_main, w_gate, *, tm, tn):
    m, d = x.shape
    n, ng = w_main.shape[2], w_gate.shape[2]
    return pl.pallas_call(
        _in_proj_kernel,
        grid=(m // tm, n // tn),
        in_specs=[pl.BlockSpec((tm, d), lambda i, j: (i, 0)),
                  pl.BlockSpec((None, 1, d), lambda i, j: (layer, 0, 0)),
                  pl.BlockSpec((None, d, tn), lambda i, j: (layer, 0, j)),
                  pl.BlockSpec((None, d, ng), lambda i, j: (layer, 0, 0))],
        out_specs=[pl.BlockSpec((tm, tn), lambda i, j: (i, j)),
                   pl.BlockSpec((tm, ng), lambda i, j: (i, 0))],
        out_shape=[jax.ShapeDtypeStruct((m, n), BF16), jax.ShapeDtypeStruct((m, ng), F32)],
        scratch_shapes=[pltpu.VMEM((tm, d), BF16)],
        compiler_params=pltpu.CompilerParams(
            dimension_semantics=("arbitrary", "arbitrary"), vmem_limit_bytes=VMEM_LIMIT),
        name="in_proj",
    )(x, g, w_main, w_gate)


def _ffn_up_kernel(x_ref, g_ref, wg_ref, wu_ref, o_ref, h_scr):
    @pl.when(pl.program_id(1) == 0)
    def _():
        h_scr[...] = _rms_rows(x_ref[...], g_ref[...]).astype(BF16)

    h = h_scr[...]
    gate = _dot(h, wg_ref[...])
    up = _dot(h, wu_ref[...])
    o_ref[...] = (_silu(gate) * up).astype(o_ref.dtype)


def _ffn_up(layer, x, g, w_up, *, tm, tn):
    m, d = x.shape
    d_ff = w_up.shape[2] // 2
    nj = d_ff // tn
    return pl.pallas_call(
        _ffn_up_kernel,
        grid=(m // tm, nj),
        in_specs=[pl.BlockSpec((tm, d), lambda i, j: (i, 0)),
                  pl.BlockSpec((None, 1, d), lambda i, j: (layer, 0, 0)),
                  pl.BlockSpec((None, d, tn), lambda i, j: (layer, 0, j)),
                  pl.BlockSpec((None, d, tn), lambda i, j: (layer, 0, j + nj))],
        out_specs=pl.BlockSpec((tm, tn), lambda i, j: (i, j)),
        out_shape=jax.ShapeDtypeStruct((m, d_ff), BF16),
        scratch_shapes=[pltpu.VMEM((tm, d), BF16)],
        compiler_params=pltpu.CompilerParams(
            dimension_semantics=("arbitrary", "arbitrary"), vmem_limit_bytes=VMEM_LIMIT),
        name="ffn_up",
    )(x, g, w_up, w_up)


def _residual_matmul_kernel(n_in, res_ref, *refs):
    a_refs, w_refs, o_ref = refs[:n_in], refs[n_in:2 * n_in], refs[2 * n_in]
    acc = res_ref[...]
    for a_ref, w_ref in zip(a_refs, w_refs):
        acc = acc + _dot(a_ref[...], w_ref[...])
    o_ref[...] = acc


def _residual_matmul(layer, res, acts, weights, *, tm):
    m, n = res.shape
    n_in = len(acts)
    in_specs = [pl.BlockSpec((tm, n), lambda i: (i, 0))]
    in_specs += [pl.BlockSpec((tm, a.shape[1]), lambda i: (i, 0)) for a in acts]
    in_specs += [pl.BlockSpec((None,) + w.shape[1:], lambda i: (layer, 0, 0)) for w in weights]
    return pl.pallas_call(
        functools.partial(_residual_matmul_kernel, n_in),
        grid=(m // tm,),
        in_specs=in_specs,
        out_specs=pl.BlockSpec((tm, n), lambda i: (i, 0)),
        out_shape=jax.ShapeDtypeStruct((m, n), F32),
        compiler_params=pltpu.CompilerParams(
            dimension_semantics=("arbitrary",), vmem_limit_bytes=VMEM_LIMIT),
        name="residual_matmul",
    )(res, *acts, *weights)


def _pair_rms(x, g, lo_lanes):
    x2 = x * x
    s_lo = jnp.sum(jnp.where(lo_lanes, x2, 0.0), axis=-1, keepdims=True)
    s_hi = jnp.sum(jnp.where(lo_lanes, 0.0, x2), axis=-1, keepdims=True)
    ms = jnp.where(lo_lanes, s_lo, s_hi) * (2.0 / LANES)
    return x * lax.rsqrt(ms + RMS_EPS) * g


def _decay_sum_matrix(chunk):
    t = np.arange(chunk)
    blocks = [(t[None, :] <= t[:, None]), (t[None, :] > t[:, None])]
    h = 1
    while h < chunk:
        mid = (t // (2 * h)) * (2 * h) + h
        right = t >= mid
        m = np.where(right[:, None],
                     (t[None, :] >= mid[:, None]) & (t[None, :] <= t[:, None]),
                     (t[None, :] > t[:, None]) & (t[None, :] < mid[:, None]))
        blocks.append(m)
        h *= 2
    return np.concatenate(blocks, axis=0).astype(np.float32)


def _pair_owner_matrix(chunk):
    t = np.arange(chunk)
    x = t[:, None] ^ t[None, :]
    level = np.floor(np.log2(np.maximum(x, 1))).astype(np.int32)
    return np.where(x == 0, 0, np.where(t[None, :] < t[:, None], 1 + level, -1)).astype(np.int32)


def _chunk_step(qb, kb, lf, v_ref, gate_ref, gain, first_chunk, msum_ref, owner_ref,
                rd, wr, state_ref, o_ref, *, heads, dk_lanes, dv, dv_lanes):
    q_rd, k_rd, dec_rd = rd
    q_wr, k_wr, dec_wr = wr
    c = qb.shape[0]
    n_levels = c.bit_length() - 1
    n_tiles = n_levels + 1
    lf_b = lf.astype(BF16)

    def decay_rows(block):
        return jnp.exp(_dot(msum_ref[block * c:(block + 1) * c, :], lf_b).astype(BF16))

    def store_slot(t):
        if t == 0:
            q_wr[0] = qb
            k_wr[0] = kb.T
        elif t <= n_levels:
            e = decay_rows(1 + t)
            q_wr[t] = qb * e
            k_wr[t] = (kb * e).T
        else:
            q_wr[t] = qb * decay_rows(0)
            k_wr[t] = (kb * decay_rows(1)).T
            total = lax.dot_general(lf_b, jnp.ones((c, LANES), BF16), (((0,), (0,)), ((), ())),
                                    preferred_element_type=F32)
            dec_wr[...] = jnp.exp(total)

    owner = owner_ref[...]
    owned = [owner == t for t in range(n_tiles)]
    lane = lax.broadcasted_iota(jnp.int32, (1, LANES), 1)

    for hd in range(heads):
        g0 = (hd * dk_lanes // LANES) * LANES
        grp = slice(g0, g0 + LANES)
        if dk_lanes < LANES:
            lo = hd * dk_lanes - g0
            head_lanes = jnp.where((lane >= lo) & (lane < lo + dk_lanes), 1.0, 0.0).astype(BF16)
            pick = lambda t: q_rd[t, :, grp] * head_lanes
        else:
            pick = lambda t: q_rd[t, :, grp]
        scores = jnp.zeros((c, c), F32)
        for t in range(n_tiles):
            scores = jnp.where(owned[t], _dot(pick(t), k_rd[t, grp, :]), scores)
        v0 = (hd * dv_lanes // LANES) * LANES
        sl = slice(v0, v0 + LANES)
        v_h = v_ref[:, sl]
        state = jnp.where(first_chunk, 0.0, state_ref[hd])
        o = _dot(scores.astype(BF16), v_h) + _dot(pick(n_tiles), state.astype(BF16))
        state_ref[hd] = state * dec_rd[grp, :] + _dot(k_rd[n_tiles, grp, :], v_h)
        if dv_lanes == LANES:
            ms = jnp.sum(o * o, axis=-1, keepdims=True) * (1.0 / dv)
            y = o * lax.rsqrt(ms + RMS_EPS) * gain
        elif hd % 2 == 0:
            o_even = o
        else:
            y = _pair_rms(jnp.where(lane < dv_lanes, o_even, o), gain, lane < dv_lanes)
        if dv_lanes == LANES or hd % 2 == 1:
            o_ref[:, sl] = (y * _silu(gate_ref[:, sl].astype(F32))).astype(o_ref.dtype)

        for t in range(hd, n_tiles + 1, heads):
            store_slot(t)
        yield


def _mixer_step(scratch, step_fn):
    @pl.when(pl.program_id(0) == 0)
    def _():
        for r in scratch:
            r[...] = jnp.zeros_like(r)

    parity = lax.rem(pl.program_id(0), 2)
    for buf in (0, 1):
        @pl.when(parity == buf)
        def _():
            step_fn(1 - buf, buf)


def _mixers_kernel(layer, n_chunks,
                   gq_ref, gk_ref, lr_ref, wd_ref, bd_ref, hq_ref, hf_ref, lb_ref,
                   gv_ref, gg_ref, gain_a_ref, hi_ref, hg_ref, gain_c_ref, msum_ref, owner_ref,
                   oa_ref, oc_ref,
                   state_a, q_a, k_a, dec_a, state_c, q_c, k_c, dec_c):
    first_chunk = lax.rem(pl.program_id(0) + n_chunks - 1, n_chunks) == 0

    def step(rd, wr):
        lr_hi, lr_lo = _split_bf16(lr_ref[...])
        wd_hi, wd_lo = _split_bf16(wd_ref[...])
        logits = _dot(lr_hi, wd_hi) + _dot(lr_hi, wd_lo) + _dot(lr_lo, wd_hi) + bd_ref[...]
        lf_a = -_softplus(-logits) * (1.0 / GLA_GATE_NORMALIZER)
        qb_a = (gq_ref[...].astype(F32) * GLA_DK ** -0.5).astype(BF16)

        lb_logits = lb_ref[...]
        ex = jnp.exp(lb_logits - jnp.max(lb_logits, axis=0, keepdims=True))
        probs = ex / jnp.sum(ex, axis=0, keepdims=True)
        lb = jnp.zeros_like(probs[0:1])
        for d in range(1, layer + 1):
            lb = lb + probs[d:d + 1]
        hf = hf_ref[...]
        sp = _softplus(-hf)
        log_sig = -sp
        a = jnp.log(jnp.maximum(lb, 1e-30))
        b = jnp.log(1.0 - lb) + log_sig
        lae = jnp.maximum(a, b) + jnp.log(1.0 + jnp.exp(-jnp.abs(a - b)))
        lf_c = jnp.where(lb > 0.0, lae, log_sig)
        kb_c = ((1.0 - lb) * jnp.exp(-(hf + sp))).astype(BF16)

        gla = _chunk_step(qb_a, gk_ref[...], lf_a, gv_ref, gg_ref, gain_a_ref[...], first_chunk,
                          msum_ref, owner_ref, (q_a.at[rd], k_a.at[rd], dec_a.at[rd]),
                          (q_a.at[wr], k_a.at[wr], dec_a.at[wr]), state_a, oa_ref,
                          heads=GLA_HEADS, dk_lanes=GLA_DK_PAD, dv=GLA_DV, dv_lanes=HEAD_V_PAD)
        hgrn = _chunk_step(hq_ref[...], kb_c, lf_c, hi_ref, hg_ref, gain_c_ref[...], first_chunk,
                           msum_ref, owner_ref, (q_c.at[rd], k_c.at[rd], dec_c.at[rd]),
                           (q_c.at[wr], k_c.at[wr], dec_c.at[wr]), state_c, oc_ref,
                           heads=HG_HEADS, dk_lanes=HG_DK, dv=HG_DV, dv_lanes=HG_DV)
        for _ in zip(gla, hgrn):
            pass

    _mixer_step((state_a, q_a, k_a, dec_a, state_c, q_c, k_c, dec_c), step)


def _mixers(layer, projs, wd, bd, gain_a, lb_logits, gain_c, *, batch):
    assert GLA_HEADS == HG_HEADS
    m = projs[0].shape[0]
    n_chunks = m // batch // MIX_CHUNK
    n_steps = m // MIX_CHUNK
    msum = jnp.asarray(_decay_sum_matrix(MIX_CHUNK), BF16)
    owner = jnp.asarray(_pair_owner_matrix(MIX_CHUNK))
    lead = lambda s: jnp.minimum(s, n_steps - 1)
    lag = lambda s: jnp.maximum(s - 1, 0)

    def seg(name, row_of):
        arr, off, width = _SEG[name]
        assert off % width == 0
        return projs[arr], pl.BlockSpec((MIX_CHUNK, width), lambda s: (row_of(s), off // width))

    const = lambda e: (e, pl.BlockSpec(e.shape, lambda s: (0, 0)))
    per_layer = lambda e: (e, pl.BlockSpec((None,) + e.shape[1:], lambda s: (layer, 0, 0)))
    operands = [seg("gq", lead), seg("gk", lead), seg("glr", lead), per_layer(wd), per_layer(bd),
                seg("hq", lead), seg("hf", lead), const(lb_logits),
                seg("gv", lag), seg("gg", lag), per_layer(gain_a),
                seg("hi", lag), seg("hg", lag), per_layer(gain_c), const(msum), const(owner)]
    widths = (GLA_HEADS * HEAD_V_PAD, HG_HEADS * HG_DV)
    n_slots = MIX_CHUNK.bit_length() + 1

    def scratch(dk_lanes):
        w = GLA_HEADS * dk_lanes
        return [pltpu.VMEM((GLA_HEADS, LANES, HEAD_V_PAD), F32),
                pltpu.VMEM((2, n_slots, MIX_CHUNK, w), BF16),
                pltpu.VMEM((2, n_slots, w, MIX_CHUNK), BF16),
                pltpu.VMEM((2, w, LANES), F32)]

    out_spec = lambda width: pl.BlockSpec((MIX_CHUNK, width), lambda s: (lag(s), 0))
    return pl.pallas_call(
        functools.partial(_mixers_kernel, layer, n_chunks),
        grid=(n_steps + 1,),
        in_specs=[spec for _, spec in operands],
        out_specs=[out_spec(w) for w in widths],
        out_shape=[jax.ShapeDtypeStruct((m, w), BF16) for w in widths],
        scratch_shapes=scratch(GLA_DK_PAD) + scratch(HG_DK),
        compiler_params=pltpu.CompilerParams(
            dimension_semantics=("arbitrary",), vmem_limit_bytes=VMEM_LIMIT),
        name="mixers",
    )(*[arr for arr, _ in operands])


def _sb_kernel(q_ref, k_ref, v_ref, qg_ref, kg_ref, og_ref, ut_ref, o_ref,
               kn_scr, vt_scr, q2t_scr, c_scr, acc_scr,
               bias_scr, sp_scr, zs_scr, sp0_scr, w_scr):
    grp = pl.program_id(2)
    bk = SB_BLOCK
    gq = q_ref.shape[0]
    n_kb = k_ref.shape[0] // bk
    lo_lanes = lax.broadcasted_iota(jnp.int32, (1, LANES), 1) < SB_DH

    @pl.when(grp == 0)
    def _():
        kn_scr[...] = _pair_rms(k_ref[...].astype(F32), kg_ref[...], lo_lanes).astype(BF16)

        def transpose_block(kb, carry):
            rows = pl.ds(pl.multiple_of(kb * bk, bk), bk)
            vt_scr[kb] = v_ref[rows, :].astype(F32).T.astype(BF16)
            return carry

        lax.fori_loop(0, n_kb, transpose_block, 0)

        key = lax.broadcasted_iota(jnp.int32, (bk, bk), 0)
        qry = lax.broadcasted_iota(jnp.int32, (bk, bk), 1)
        bias_scr[...] = jnp.where(key < qry, 0.0, MASK_BIAS)

    qn = _pair_rms(q_ref[...].astype(F32), qg_ref[...], lo_lanes) * (SB_DH ** -0.5 * LOG2_E)
    q2t_scr[:, :gq] = jnp.where(lo_lanes, qn, 0.0).T.astype(BF16)
    q2t_scr[:, gq:] = jnp.where(lo_lanes, 0.0, qn).T.astype(BF16)
    acc_scr[...] = jnp.zeros_like(acc_scr)
    c_scr[...] = jnp.zeros_like(c_scr)

    last = grp * SB_QGROUP + SB_QGROUP - 1

    def score_columns(kt, u, cols, diagonal):
        z = _dot(kt, q2t_scr[:, cols])
        if diagonal:
            parts = [z[:, :bk] + bias_scr[...]] + ([z[:, bk:]] if z.shape[1] > bk else [])
            z = jnp.concatenate(parts, axis=1)
        sp = jnp.maximum(z, 0.0) + jnp.log2(1.0 + jnp.exp2(-jnp.abs(z)))
        sp_scr[u, :, cols] = sp.astype(BF16)
        zs_scr[u, :, cols] = z - sp
        sp0_scr[u, :, cols] = sp[0:8]

    def stage_scores(p, masked):
        for u in range(SB_UNROLL):
            m = p * SB_UNROLL + u
            kt = kn_scr[pl.ds(pl.multiple_of((last - m) * bk, bk), bk), :]
            if not masked:
                score_columns(kt, u, slice(0, 2 * gq), False)
                continue
            lo = (SB_QGROUP - 1 - m) * bk
            for half in (0, gq):
                score_columns(kt, u, slice(half + lo, half + gq), True)
                if lo:
                    hidden = slice(half, half + lo)
                    sp_scr[u, :, hidden] = jnp.zeros((bk, lo), BF16)
                    zs_scr[u, :, hidden] = jnp.full((bk, lo), MASK_BIAS, F32)
                    sp0_scr[u, :, hidden] = jnp.zeros((8, lo), F32)

    def stage_weights(p):
        c = c_scr[...]
        for u in range(SB_UNROLL):
            later = _dot(ut_ref[...], sp_scr[u])
            w_scr[u] = jnp.exp2(zs_scr[u] - later - c).astype(BF16)
            c = c + later[0:1] + sp0_scr[u, 0:1, :]
        c_scr[...] = c

    def stage_values(p):
        for u in range(SB_UNROLL):
            vt = vt_scr[last - (p * SB_UNROLL + u)]
            w = w_scr[u]
            acc_scr[0:SB_DH, :] += _dot(vt[0:SB_DH], w[:, :gq])
            acc_scr[SB_DH:, :] += _dot(vt[SB_DH:], w[:, gq:])

    n_masked = SB_QGROUP // SB_UNROLL
    for p in range(n_masked):
        if p >= 2:
            stage_values(p - 2)
        if p >= 1:
            stage_weights(p - 1)
        stage_scores(p, True)

    n_trips = grp * (n_masked // SB_LOOP_STEPS)

    def keep_going(carry):
        trip, c_min = carry
        return jnp.logical_and(trip < n_trips, c_min < SB_DEAD_LOG2)

    def body(carry):
        trip, _ = carry
        for s in range(SB_LOOP_STEPS):
            p = n_masked + trip * SB_LOOP_STEPS + s
            stage_values(p - 2)
            stage_weights(p - 1)
            stage_scores(p, False)
        return trip + 1, jnp.min(c_scr[...])

    trips, c_min = lax.while_loop(keep_going, body, (jnp.int32(0), jnp.float32(0.0)))
    p_end = n_masked + trips * SB_LOOP_STEPS
    stage_values(p_end - 2)

    @pl.when(c_min < SB_DEAD_LOG2)
    def _():
        stage_weights(p_end - 1)
        stage_values(p_end - 1)

    o_ref[...] = _pair_rms(acc_scr[...].T, og_ref[...], lo_lanes).astype(o_ref.dtype)


def _sb_attention(proj, qg, kg, og, *, batch):
    m = proj.shape[0]
    t = m // batch
    bk = SB_BLOCK
    gq = SB_QGROUP * bk
    n_groups = t // gq
    n_pairs = SB_HEADS * SB_DH // LANES
    oq, ok, ov = (_SEG[s][1] // LANES for s in ("sq", "sk", "sv"))
    j = np.arange(bk)
    ut = jnp.asarray((j[None, :] > j[:, None]).astype(np.float32), BF16)
    pair = lambda g: jnp.tile(g, 2).reshape(1, LANES)
    const = lambda shape: pl.BlockSpec(shape, lambda b, p, i: (0, 0))
    return pl.pallas_call(
        _sb_kernel,
        grid=(batch, n_pairs, n_groups),
        in_specs=[pl.BlockSpec((gq, LANES), lambda b, p, i: (b * n_groups + i, oq + p)),
                  pl.BlockSpec((t, LANES), lambda b, p, i: (b, ok + p)),
                  pl.BlockSpec((t, LANES), lambda b, p, i: (b, ov + p)),
                  const((1, LANES)), const((1, LANES)), const((1, LANES)),
                  const((bk, bk))],
        out_specs=pl.BlockSpec((gq, LANES), lambda b, p, i: (b * n_groups + i, p)),
        out_shape=jax.ShapeDtypeStruct((m, n_pairs * LANES), BF16),
        scratch_shapes=[pltpu.VMEM((t, LANES), BF16),
                        pltpu.VMEM((t // bk, LANES, bk), BF16),
                        pltpu.VMEM((LANES, 2 * gq), BF16),
                        pltpu.VMEM((1, 2 * gq), F32),
                        pltpu.VMEM((LANES, gq), F32),
                        pltpu.VMEM((bk, bk), F32),
                        pltpu.VMEM((SB_UNROLL, bk, 2 * gq), BF16),
                        pltpu.VMEM((SB_UNROLL, bk, 2 * gq), F32),
                        pltpu.VMEM((SB_UNROLL, 8, 2 * gq), F32),
                        pltpu.VMEM((SB_UNROLL, bk, 2 * gq), BF16)],
        compiler_params=pltpu.CompilerParams(
            dimension_semantics=("arbitrary", "arbitrary", "arbitrary"),
            vmem_limit_bytes=VMEM_LIMIT),
        name="sb_attention",
    )(proj, proj, proj, pair(qg), pair(kg), pair(og), ut)


def kernel(x, norm_mix_g, w_in, gla_w_decay, gla_b_decay, gla_out_g, sb_q_g, sb_k_g, sb_out_g,
           hg_out_g, hg_lb_logits, w_out, norm_ffn_g, w_ffn_up, w_ffn_down):
    batch, seq, d_model = x.shape
    depth = w_in.shape[0]
    x2 = x.reshape(batch * seq, d_model).astype(F32)

    gla_cols = _padded_head_cols(0, GLA_HEADS, GLA_DK, GLA_DK_PAD)
    sb_lo = GLA_HEADS * GLA_DV
    hg_lo = sb_lo + SB_HEADS * SB_DH
    w_main = _take_padded(w_in, _MAIN_SRC, 2).astype(BF16)
    w_gate = _take_padded(w_in, _GATE_SRC, 2).astype(BF16)
    wd = _take_padded(gla_w_decay.astype(F32), gla_cols, 2)
    wd = jnp.pad(wd, ((0, 0), (0, LANES - GLA_LOWRANK), (0, 0)))
    bd = _take_padded(gla_b_decay.astype(F32), gla_cols, 1)[:, None, :]
    gain_a = jnp.pad(gla_out_g.astype(F32), ((0, 0), (0, HEAD_V_PAD - GLA_DV)))[:, None, :]
    gain_c = jnp.tile(hg_out_g.astype(F32), (1, LANES // HG_DV))[:, None, :]
    w_a = _take_padded(w_out, _padded_head_cols(0, GLA_HEADS, GLA_DV, HEAD_V_PAD), 1).astype(BF16)
    w_b = w_out[:, sb_lo:hg_lo].astype(BF16)
    w_c = w_out[:, hg_lo:].astype(BF16)
    w_up = w_ffn_up.astype(BF16)
    w_down = w_ffn_down.astype(BF16)
    lb_logits = hg_lb_logits.astype(F32)
    g_mix = norm_mix_g.astype(F32)[:, None, :]
    g_ffn = norm_ffn_g.astype(F32)[:, None, :]

    for li in range(depth):
        projs = _in_proj(li, x2, g_mix, w_main, w_gate, tm=1024, tn=MAIN_COLS // 3)
        o_a, o_c = _mixers(li, projs, wd, bd, gain_a, lb_logits, gain_c, batch=batch)
        o_b = _sb_attention(projs[0], sb_q_g[li].astype(F32), sb_k_g[li].astype(F32),
                            sb_out_g[li].astype(F32), batch=batch)
        x2 = _residual_matmul(li, x2, (o_a, o_b, o_c), (w_a, w_b, w_c), tm=1024)
        act = _ffn_up(li, x2, g_ffn, w_up, tm=1024, tn=1408)
        x2 = _residual_matmul(li, x2, (act,), (w_down,), tm=512)
    return x2.reshape(batch, seq, d_model).astype(x.dtype)
```

```python
import functools

import numpy as np
import jax
import jax.numpy as jnp
from jax import lax
from jax.experimental import pallas as pl
from jax.experimental.pallas import tpu as pltpu

F32 = jnp.float32
BF16 = jnp.bfloat16

LANES = 128
RMS_EPS = 1e-6

GLA_HEADS, GLA_DK, GLA_DV, GLA_LOWRANK = 4, 48, 96, 16
GLA_GATE_NORMALIZER = 16.0
SB_HEADS, SB_DH, SB_BLOCK = 6, 64, 128
HG_HEADS, HG_DK, HG_DV = 4, 128, 64

GLA_DK_PAD = 64
HEAD_V_PAD = LANES
MIX_CHUNK = 128
SB_QGROUP = 4
SB_UNROLL = 2
SB_LOOP_STEPS = 1
SB_SEQS = 2
SB_DEAD_LOG2 = 150.0
LOG2_E = 1.4426950408889634
MASK_BIAS = -1e30
assert SB_QGROUP % SB_UNROLL == 0 and SB_QGROUP // SB_UNROLL >= 2
assert (SB_QGROUP // SB_UNROLL) % SB_LOOP_STEPS == 0
ROW_TILE = 1024
VMEM_LIMIT = 48 * 1024 * 1024

_MAIN_SEGS = (("gv", 512), ("gg", 512), ("hq", 512), ("hi", 256), ("hg", 256),
              ("gq", 256), ("gk", 256), ("sq", 384), ("sk", 384), ("sv", 384), ("pad", 128))
_GATE_SEGS = (("hf", 512), ("glr", 128))
_SEG = {}
for _arr, _segs in enumerate((_MAIN_SEGS, _GATE_SEGS)):
    _off = 0
    for _name, _w in _segs:
        _SEG[_name] = (_arr, _off, _w)
        _off += _w
MAIN_COLS = sum(w for _, w in _MAIN_SEGS)
GATE_COLS = sum(w for _, w in _GATE_SEGS)


def _padded_head_cols(start, heads, width, pad):
    idx = -np.ones((heads, pad), np.int64)
    idx[:, :width] = start + np.arange(heads)[:, None] * width + np.arange(width)[None, :]
    return idx.reshape(-1)


def _proj_source_columns():
    sizes = (GLA_HEADS * GLA_DK, GLA_HEADS * GLA_DK, GLA_HEADS * GLA_DV, GLA_LOWRANK,
             GLA_HEADS * GLA_DV, SB_HEADS * SB_DH, SB_HEADS * SB_DH, SB_HEADS * SB_DH,
             HG_HEADS * HG_DK, HG_HEADS * HG_DK, HG_HEADS * HG_DV, HG_HEADS * HG_DV)
    starts = np.concatenate([[0], np.cumsum(sizes)[:-1]])
    (gq, gk, gv, glr, gg, sq, sk, sv, hq, hf, hi, hg) = [int(s) for s in starts]
    src = {
        "gq": _padded_head_cols(gq, GLA_HEADS, GLA_DK, GLA_DK_PAD),
        "gk": _padded_head_cols(gk, GLA_HEADS, GLA_DK, GLA_DK_PAD),
        "gv": _padded_head_cols(gv, GLA_HEADS, GLA_DV, HEAD_V_PAD),
        "gg": _padded_head_cols(gg, GLA_HEADS, GLA_DV, HEAD_V_PAD),
        "glr": _padded_head_cols(glr, 1, GLA_LOWRANK, LANES),
        "sq": np.arange(sq, sq + SB_HEADS * SB_DH),
        "sk": np.arange(sk, sk + SB_HEADS * SB_DH),
        "sv": np.arange(sv, sv + SB_HEADS * SB_DH),
        "hq": np.arange(hq, hq + HG_HEADS * HG_DK),
        "hf": np.arange(hf, hf + HG_HEADS * HG_DK),
        "hi": np.arange(hi, hi + HG_HEADS * HG_DV),
        "hg": np.arange(hg, hg + HG_HEADS * HG_DV),
        "pad": -np.ones(LANES, np.int64),
    }
    return tuple(np.concatenate([src[name] for name, _ in segs]) for segs in (_MAIN_SEGS, _GATE_SEGS))


_MAIN_SRC, _GATE_SRC = _proj_source_columns()


def _take_padded(arr, src, axis):
    pad = src < 0
    breaks = np.flatnonzero(np.where(pad[1:] | pad[:-1], pad[1:] != pad[:-1], np.diff(src) != 1)) + 1
    pieces = []
    for run in np.split(src, breaks):
        if run[0] < 0:
            shape = list(arr.shape)
            shape[axis] = len(run)
            pieces.append(jnp.zeros(shape, arr.dtype))
        else:
            pieces.append(lax.slice_in_dim(arr, int(run[0]), int(run[-1]) + 1, axis=axis))
    return jnp.concatenate(pieces, axis=axis)


def _dot(a, b):
    return jnp.dot(a, b, preferred_element_type=F32)


def _split_bf16(x):
    hi = x.astype(BF16)
    lo = (x - hi.astype(F32)).astype(BF16)
    return hi, lo


def _softplus(z):
    return jnp.maximum(z, 0.0) + jnp.log(1.0 + jnp.exp(-jnp.abs(z)))


def _silu(z):
    return z / (1.0 + jnp.exp(-z))


def _rms_rows(x, g):
    ms = jnp.mean(x * x, axis=-1, keepdims=True)
    return x * lax.rsqrt(ms + RMS_EPS) * g


def _in_proj_kernel(x_ref, g_ref, w_ref, wg_ref, o_ref, og_ref, h_scr):
    @pl.when(pl.program_id(1) == 0)
    def _():
        h = _rms_rows(x_ref[...], g_ref[...]).astype(BF16)
        h_scr[...] = h
        og_ref[...] = _dot(h, wg_ref[...])

    o_ref[...] = _dot(h_scr[...], w_ref[...]).astype(o_ref.dtype)


def _in_proj(layer, x, g, w_main, w_gate, *, tm, tn):
    m, d = x.shape
    n, ng = w_main.shape[2], w_gate.shape[2]
    return pl.pallas_call(
        _in_proj_kernel,
        grid=(m // tm, n // tn),
        in_specs=[pl.BlockSpec((tm, d), lambda i, j: (i, 0)),
                  pl.BlockSpec((None, 1, d), lambda i, j: (layer, 0, 0)),
                  pl.BlockSpec((None, d, tn), lambda i, j: (layer, 0, j)),
                  pl.BlockSpec((None, d, ng), lambda i, j: (layer, 0, 0))],
        out_specs=[pl.BlockSpec((tm, tn), lambda i, j: (i, j)),
                   pl.BlockSpec((tm, ng), lambda i, j: (i, 0))],
        out_shape=[jax.ShapeDtypeStruct((m, n), BF16), jax.ShapeDtypeStruct((m, ng), F32)],
        scratch_shapes=[pltpu.VMEM((tm, d), BF16)],
        compiler_params=pltpu.CompilerParams(
            dimension_semantics=("arbitrary", "arbitrary"), vmem_limit_bytes=VMEM_LIMIT),
        name="in_proj",
    )(x, g, w_main, w_gate)


def _ffn_up_kernel(x_ref, g_ref, wg_ref, wu_ref, o_ref, h_scr):
    @pl.when(pl.program_id(1) == 0)
    def _():
        h_scr[...] = _rms_rows(x_ref[...], g_ref[...]).astype(BF16)

    h = h_scr[...]
    gate = _dot(h, wg_ref[...])
    up = _dot(h, wu_ref[...])
    o_ref[...] = (_silu(gate) * up).astype(o_ref.dtype)


def _ffn_up(layer, x, g, w_up, *, tm, tn):
    m, d = x.shape
    d_ff = w_up.shape[2] // 2
    nj = d_ff // tn
    return pl.pallas_call(
        _ffn_up_kernel,
        grid=(m // tm, nj),
        in_specs=[pl.BlockSpec((tm, d), lambda i, j: (i, 0)),
                  pl.BlockSpec((None, 1, d), lambda i, j: (layer, 0, 0)),
                  pl.BlockSpec((None, d, tn), lambda i, j: (layer, 0, j)),
                  pl.BlockSpec((None, d, tn), lambda i, j: (layer, 0, j + nj))],
        out_specs=pl.BlockSpec((tm, tn), lambda i, j: (i, j)),
        out_shape=jax.ShapeDtypeStruct((m, d_ff), BF16),
        scratch_shapes=[pltpu.VMEM((tm, d), BF16)],
        compiler_params=pltpu.CompilerParams(
            dimension_semantics=("arbitrary", "arbitrary"), vmem_limit_bytes=VMEM_LIMIT),
        name="ffn_up",
    )(x, g, w_up, w_up)


def _residual_matmul_kernel(n_in, res_ref, *refs):
    a_refs, w_refs, o_ref = refs[:n_in], refs[n_in:2 * n_in], refs[2 * n_in]
    acc = res_ref[...]
    for a_ref, w_ref in zip(a_refs, w_refs):
        acc = acc + _dot(a_ref[...], w_ref[...])
    o_ref[...] = acc


def _residual_matmul(layer, res, acts, weights, *, tm):
    m, n = res.shape
    n_in = len(acts)
    in_specs = [pl.BlockSpec((tm, n), lambda i: (i, 0))]
    in_specs += [pl.BlockSpec((tm, a.shape[1]), lambda i: (i, 0)) for a in acts]
    in_specs += [pl.BlockSpec((None,) + w.shape[1:], lambda i: (layer, 0, 0)) for w in weights]
    return pl.pallas_call(
        functools.partial(_residual_matmul_kernel, n_in),
        grid=(m // tm,),
        in_specs=in_specs,
        out_specs=pl.BlockSpec((tm, n), lambda i: (i, 0)),
        out_shape=jax.ShapeDtypeStruct((m, n), F32),
        compiler_params=pltpu.CompilerParams(
            dimension_semantics=("arbitrary",), vmem_limit_bytes=VMEM_LIMIT),
        name="residual_matmul",
    )(res, *acts, *weights)


def _pair_rms(x, g, lo_lanes):
    x2 = x * x
    s_lo = jnp.sum(jnp.where(lo_lanes, x2, 0.0), axis=-1, keepdims=True)
    s_hi = jnp.sum(jnp.where(lo_lanes, 0.0, x2), axis=-1, keepdims=True)
    ms = jnp.where(lo_lanes, s_lo, s_hi) * (2.0 / LANES)
    return x * lax.rsqrt(ms + RMS_EPS) * g


def _decay_sum_matrix(chunk):
    t = np.arange(chunk)
    blocks = [(t[None, :] <= t[:, None]), (t[None, :] > t[:, None])]
    h = 1
    while h < chunk:
        mid = (t // (2 * h)) * (2 * h) + h
        right = t >= mid
        m = np.where(right[:, None],
                     (t[None, :] >= mid[:, None]) & (t[None, :] <= t[:, None]),
                     (t[None, :] > t[:, None]) & (t[None, :] < mid[:, None]))
        blocks.append(m)
        h *= 2
    return np.concatenate(blocks, axis=0).astype(np.float32)


def _pair_owner_matrix(chunk):
    t = np.arange(chunk)
    x = t[:, None] ^ t[None, :]
    level = np.floor(np.log2(np.maximum(x, 1))).astype(np.int32)
    return np.where(x == 0, 0, np.where(t[None, :] < t[:, None], 1 + level, -1)).astype(np.int32)


def _chunk_step(qb, kb, lf, v_ref, gate_ref, gain, first_chunk, msum_ref, owner_ref,
                rd, wr, state_ref, o_ref, *, heads, dk_lanes, dv, dv_lanes):
    q_rd, k_rd, dec_rd = rd
    q_wr, k_wr, dec_wr = wr
    c = qb.shape[0]
    n_levels = c.bit_length() - 1
    n_tiles = n_levels + 1
    lf_b = lf.astype(BF16)

    def decay_rows(block):
        return jnp.exp(_dot(msum_ref[block * c:(block + 1) * c, :], lf_b).astype(BF16))

    def store_slot(t):
        if t == 0:
            q_wr[0] = qb
            k_wr[0] = kb.T
        elif t <= n_levels:
            e = decay_rows(1 + t)
            q_wr[t] = qb * e
            k_wr[t] = (kb * e).T
        else:
            q_wr[t] = qb * decay_rows(0)
            k_wr[t] = (kb * decay_rows(1)).T
            total = lax.dot_general(lf_b, jnp.ones((c, LANES), BF16), (((0,), (0,)), ((), ())),
                                    preferred_element_type=F32)
            dec_wr[...] = jnp.exp(total)

    owner = owner_ref[...]
    owned = [owner == t for t in range(n_tiles)]
    lane = lax.broadcasted_iota(jnp.int32, (1, LANES), 1)

    for hd in range(heads):
        g0 = (hd * dk_lanes // LANES) * LANES
        grp = slice(g0, g0 + LANES)
        if dk_lanes < LANES:
            lo = hd * dk_lanes - g0
            head_lanes = jnp.where((lane >= lo) & (lane < lo + dk_lanes), 1.0, 0.0).astype(BF16)
            pick = lambda t: q_rd[t, :, grp] * head_lanes
        else:
            pick = lambda t: q_rd[t, :, grp]
        scores = jnp.zeros((c, c), F32)
        for t in range(n_tiles):
            scores = jnp.where(owned[t], _dot(pick(t), k_rd[t, grp, :]), scores)
        v0 = (hd * dv_lanes // LANES) * LANES
        sl = slice(v0, v0 + LANES)
        v_h = v_ref[:, sl]
        state = jnp.where(first_chunk, 0.0, state_ref[hd])
        o = _dot(scores.astype(BF16), v_h) + _dot(pick(n_tiles), state.astype(BF16))
        state_ref[hd] = state * dec_rd[grp, :] + _dot(k_rd[n_tiles, grp, :], v_h)
        if dv_lanes == LANES:
            ms = jnp.sum(o * o, axis=-1, keepdims=True) * (1.0 / dv)
            y = o * lax.rsqrt(ms + RMS_EPS) * gain
        elif hd % 2 == 0:
            o_even = o
        else:
            y = _pair_rms(jnp.where(lane < dv_lanes, o_even, o), gain, lane < dv_lanes)
        if dv_lanes == LANES or hd % 2 == 1:
            o_ref[:, sl] = (y * _silu(gate_ref[:, sl].astype(F32))).astype(o_ref.dtype)

        for t in range(hd, n_tiles + 1, heads):
            store_slot(t)
        yield


def _mixer_step(scratch, step_fn):
    @pl.when(pl.program_id(0) == 0)
    def _():
        for r in scratch:
            r[...] = jnp.zeros_like(r)

    parity = lax.rem(pl.program_id(0), 2)
    for buf in (0, 1):
        @pl.when(parity == buf)
        def _():
            step_fn(1 - buf, buf)


def _mixers_kernel(layer, n_chunks,
                   gq_ref, gk_ref, lr_ref, wd_ref, bd_ref, hq_ref, hf_ref, lb_ref,
                   gv_ref, gg_ref, gain_a_ref, hi_ref, hg_ref, gain_c_ref, msum_ref, owner_ref,
                   oa_ref, oc_ref,
                   state_a, q_a, k_a, dec_a, state_c, q_c, k_c, dec_c):
    first_chunk = lax.rem(pl.program_id(0) + n_chunks - 1, n_chunks) == 0

    def step(rd, wr):
        lr_hi, lr_lo = _split_bf16(lr_ref[...])
        wd_hi, wd_lo = _split_bf16(wd_ref[...])
        logits = _dot(lr_hi, wd_hi) + _dot(lr_hi, wd_lo) + _dot(lr_lo, wd_hi) + bd_ref[...]
        lf_a = -_softplus(-logits) * (1.0 / GLA_GATE_NORMALIZER)
        qb_a = (gq_ref[...].astype(F32) * GLA_DK ** -0.5).astype(BF16)

        lb_logits = lb_ref[...]
        ex = jnp.exp(lb_logits - jnp.max(lb_logits, axis=0, keepdims=True))
        probs = ex / jnp.sum(ex, axis=0, keepdims=True)
        lb = jnp.zeros_like(probs[0:1])
        for d in range(1, layer + 1):
            lb = lb + probs[d:d + 1]
        hf = hf_ref[...]
        sp = _softplus(-hf)
        log_sig = -sp
        a = jnp.log(jnp.maximum(lb, 1e-30))
        b = jnp.log(1.0 - lb) + log_sig
        lae = jnp.maximum(a, b) + jnp.log(1.0 + jnp.exp(-jnp.abs(a - b)))
        lf_c = jnp.where(lb > 0.0, lae, log_sig)
        kb_c = ((1.0 - lb) * jnp.exp(-(hf + sp))).astype(BF16)

        gla = _chunk_step(qb_a, gk_ref[...], lf_a, gv_ref, gg_ref, gain_a_ref[...], first_chunk,
                          msum_ref, owner_ref, (q_a.at[rd], k_a.at[rd], dec_a.at[rd]),
                          (q_a.at[wr], k_a.at[wr], dec_a.at[wr]), state_a, oa_ref,
                          heads=GLA_HEADS, dk_lanes=GLA_DK_PAD, dv=GLA_DV, dv_lanes=HEAD_V_PAD)
        hgrn = _chunk_step(hq_ref[...], kb_c, lf_c, hi_ref, hg_ref, gain_c_ref[...], first_chunk,
                           msum_ref, owner_ref, (q_c.at[rd], k_c.at[rd], dec_c.at[rd]),
                           (q_c.at[wr], k_c.at[wr], dec_c.at[wr]), state_c, oc_ref,
                           heads=HG_HEADS, dk_lanes=HG_DK, dv=HG_DV, dv_lanes=HG_DV)
        for _ in zip(gla, hgrn):
            pass

    _mixer_step((state_a, q_a, k_a, dec_a, state_c, q_c, k_c, dec_c), step)


def _mixers(layer, projs, wd, bd, gain_a, lb_logits, gain_c, *, batch):
    assert GLA_HEADS == HG_HEADS
    m = projs[0].shape[0]
    n_chunks = m // batch // MIX_CHUNK
    n_steps = m // MIX_CHUNK
    msum = jnp.asarray(_decay_sum_matrix(MIX_CHUNK), BF16)
    owner = jnp.asarray(_pair_owner_matrix(MIX_CHUNK))
    lead = lambda s: jnp.minimum(s, n_steps - 1)
    lag = lambda s: jnp.maximum(s - 1, 0)

    def seg(name, row_of):
        arr, off, width = _SEG[name]
        assert off % width == 0
        return projs[arr], pl.BlockSpec((MIX_CHUNK, width), lambda s: (row_of(s), off // width))

    const = lambda e: (e, pl.BlockSpec(e.shape, lambda s: (0, 0)))
    per_layer = lambda e: (e, pl.BlockSpec((None,) + e.shape[1:], lambda s: (layer, 0, 0)))
    operands = [seg("gq", lead), seg("gk", lead), seg("glr", lead), per_layer(wd), per_layer(bd),
                seg("hq", lead), seg("hf", lead), const(lb_logits),
                seg("gv", lag), seg("gg", lag), per_layer(gain_a),
                seg("hi", lag), seg("hg", lag), per_layer(gain_c), const(msum), const(owner)]
    widths = (GLA_HEADS * HEAD_V_PAD, HG_HEADS * HG_DV)
    n_slots = MIX_CHUNK.bit_length() + 1

    def scratch(dk_lanes):
        w = GLA_HEADS * dk_lanes
        return [pltpu.VMEM((GLA_HEADS, LANES, HEAD_V_PAD), F32),
                pltpu.VMEM((2, n_slots, MIX_CHUNK, w), BF16),
                pltpu.VMEM((2, n_slots, w, MIX_CHUNK), BF16),
                pltpu.VMEM((2, w, LANES), F32)]

    out_spec = lambda width: pl.BlockSpec((MIX_CHUNK, width), lambda s: (lag(s), 0))
    return pl.pallas_call(
        functools.partial(_mixers_kernel, layer, n_chunks),
        grid=(n_steps + 1,),
        in_specs=[spec for _, spec in operands],
        out_specs=[out_spec(w) for w in widths],
        out_shape=[jax.ShapeDtypeStruct((m, w), BF16) for w in widths],
        scratch_shapes=scratch(GLA_DK_PAD) + scratch(HG_DK),
        compiler_params=pltpu.CompilerParams(
            dimension_semantics=("arbitrary",), vmem_limit_bytes=VMEM_LIMIT),
        name="mixers",
    )(*[arr for arr, _ in operands])


def _sb_kernel(q_ref, k_ref, v_ref, qg_ref, kg_ref, og_ref, ut_ref, o_ref,
               kn_scr, vt_scr, q2t_scr, c_scr, acc_scr,
               bias_scr, sp_scr, zs_scr, sp0_scr, w_scr):
    grp = pl.program_id(2)
    bk = SB_BLOCK
    seqs = range(q_ref.shape[0])
    gq = q_ref.shape[1]
    n_kb = k_ref.shape[1] // bk
    lo_lanes = lax.broadcasted_iota(jnp.int32, (1, LANES), 1) < SB_DH

    @pl.when(grp == 0)
    def _():
        for sq in seqs:
            kn_scr[sq] = _pair_rms(k_ref[sq].astype(F32), kg_ref[...], lo_lanes).astype(BF16)

        def transpose_block(kb, carry):
            rows = pl.ds(pl.multiple_of(kb * bk, bk), bk)
            for sq in seqs:
                vt_scr[sq, kb] = v_ref[sq, rows, :].astype(F32).T.astype(BF16)
            return carry

        lax.fori_loop(0, n_kb, transpose_block, 0)

        key = lax.broadcasted_iota(jnp.int32, (bk, bk), 0)
        qry = lax.broadcasted_iota(jnp.int32, (bk, bk), 1)
        bias_scr[...] = jnp.where(key < qry, 0.0, MASK_BIAS)

    for sq in seqs:
        qn = _pair_rms(q_ref[sq].astype(F32), qg_ref[...], lo_lanes) * (SB_DH ** -0.5 * LOG2_E)
        q2t_scr[sq, :, :gq] = jnp.where(lo_lanes, qn, 0.0).T.astype(BF16)
        q2t_scr[sq, :, gq:] = jnp.where(lo_lanes, 0.0, qn).T.astype(BF16)
    acc_scr[...] = jnp.zeros_like(acc_scr)
    c_scr[...] = jnp.zeros_like(c_scr)

    last = grp * SB_QGROUP + SB_QGROUP - 1

    def score_columns(sq, kt, u, cols, diagonal):
        z = _dot(kt, q2t_scr[sq, :, cols])
        if diagonal:
            parts = [z[:, :bk] + bias_scr[...]] + ([z[:, bk:]] if z.shape[1] > bk else [])
            z = jnp.concatenate(parts, axis=1)
        sp = jnp.maximum(z, 0.0) + jnp.log2(1.0 + jnp.exp2(-jnp.abs(z)))
        sp_scr[sq, u, :, cols] = sp.astype(BF16)
        zs_scr[sq, u, :, cols] = z - sp
        sp0_scr[sq, u, :, cols] = sp[0:8]

    def stage_scores(p, masked):
        for u in range(SB_UNROLL):
            m = p * SB_UNROLL + u
            for sq in seqs:
                kt = kn_scr[sq, pl.ds(pl.multiple_of((last - m) * bk, bk), bk), :]
                if not masked:
                    score_columns(sq, kt, u, slice(0, 2 * gq), False)
                    continue
                lo = (SB_QGROUP - 1 - m) * bk
                for half in (0, gq):
                    score_columns(sq, kt, u, slice(half + lo, half + gq), True)
                    if lo:
                        hidden = slice(half, half + lo)
                        sp_scr[sq, u, :, hidden] = jnp.zeros((bk, lo), BF16)
                        zs_scr[sq, u, :, hidden] = jnp.full((bk, lo), MASK_BIAS, F32)
                        sp0_scr[sq, u, :, hidden] = jnp.zeros((8, lo), F32)

    def stage_weights(p):
        c = [c_scr[sq] for sq in seqs]
        for u in range(SB_UNROLL):
            for sq in seqs:
                later = _dot(ut_ref[...], sp_scr[sq, u])
                w_scr[sq, u] = jnp.exp2(zs_scr[sq, u] - later - c[sq]).astype(BF16)
                c[sq] = c[sq] + later[0:1] + sp0_scr[sq, u, 0:1, :]
        for sq in seqs:
            c_scr[sq] = c[sq]

    def stage_values(p):
        for u in range(SB_UNROLL):
            for sq in seqs:
                vt = vt_scr[sq, last - (p * SB_UNROLL + u)]
                w = w_scr[sq, u]
                acc_scr[sq, 0:SB_DH, :] += _dot(vt[0:SB_DH], w[:, :gq])
                acc_scr[sq, SB_DH:, :] += _dot(vt[SB_DH:], w[:, gq:])

    n_masked = SB_QGROUP // SB_UNROLL
    for p in range(n_masked):
        if p >= 2:
            stage_values(p - 2)
        if p >= 1:
            stage_weights(p - 1)
        stage_scores(p, True)

    n_trips = grp * (n_masked // SB_LOOP_STEPS)

    def keep_going(carry):
        trip, c_min = carry
        return jnp.logical_and(trip < n_trips, c_min < SB_DEAD_LOG2)

    def body(carry):
        trip, _ = carry
        for s in range(SB_LOOP_STEPS):
            p = n_masked + trip * SB_LOOP_STEPS + s
            stage_values(p - 2)
            stage_weights(p - 1)
            stage_scores(p, False)
        return trip + 1, jnp.min(c_scr[...])

    trips, c_min = lax.while_loop(keep_going, body, (jnp.int32(0), jnp.float32(0.0)))
    p_end = n_masked + trips * SB_LOOP_STEPS
    stage_values(p_end - 2)

    @pl.when(c_min < SB_DEAD_LOG2)
    def _():
        stage_weights(p_end - 1)
        stage_values(p_end - 1)

    for sq in seqs:
        o_ref[sq] = _pair_rms(acc_scr[sq].T, og_ref[...], lo_lanes).astype(o_ref.dtype)


def _sb_attention(proj, qg, kg, og, *, batch):
    m, cols = proj.shape
    t = m // batch
    bk = SB_BLOCK
    gq = SB_QGROUP * bk
    n_groups = t // gq
    n_pairs = SB_HEADS * SB_DH // LANES
    ns = SB_SEQS
    oq, ok, ov = (_SEG[s][1] // LANES for s in ("sq", "sk", "sv"))
    j = np.arange(bk)
    ut = jnp.asarray((j[None, :] > j[:, None]).astype(np.float32), BF16)
    pair = lambda g: jnp.tile(g, 2).reshape(1, LANES)
    const = lambda shape: pl.BlockSpec(shape, lambda b, p, i: (0, 0))
    proj3 = proj.reshape(batch, t, cols)
    out = pl.pallas_call(
        _sb_kernel,
        grid=(batch // ns, n_pairs, n_groups),
        in_specs=[pl.BlockSpec((ns, gq, LANES), lambda b, p, i: (b, i, oq + p)),
                  pl.BlockSpec((ns, t, LANES), lambda b, p, i: (b, 0, ok + p)),
                  pl.BlockSpec((ns, t, LANES), lambda b, p, i: (b, 0, ov + p)),
                  const((1, LANES)), const((1, LANES)), const((1, LANES)),
                  const((bk, bk))],
        out_specs=pl.BlockSpec((ns, gq, LANES), lambda b, p, i: (b, i, p)),
        out_shape=jax.ShapeDtypeStruct((batch, t, n_pairs * LANES), BF16),
        scratch_shapes=[pltpu.VMEM((ns, t, LANES), BF16),
                        pltpu.VMEM((ns, t // bk, LANES, bk), BF16),
                        pltpu.VMEM((ns, LANES, 2 * gq), BF16),
                        pltpu.VMEM((ns, 1, 2 * gq), F32),
                        pltpu.VMEM((ns, LANES, gq), F32),
                        pltpu.VMEM((bk, bk), F32),
                        pltpu.VMEM((ns, SB_UNROLL, bk, 2 * gq), BF16),
                        pltpu.VMEM((ns, SB_UNROLL, bk, 2 * gq), F32),
                        pltpu.VMEM((ns, SB_UNROLL, 8, 2 * gq), F32),
                        pltpu.VMEM((ns, SB_UNROLL, bk, 2 * gq), BF16)],
        compiler_params=pltpu.CompilerParams(
            dimension_semantics=("arbitrary", "arbitrary", "arbitrary"),
            vmem_limit_bytes=VMEM_LIMIT),
        name="sb_attention",
    )(proj3, proj3, proj3, pair(qg), pair(kg), pair(og), ut)
    return out.reshape(m, n_pairs * LANES)


def kernel(x, norm_mix_g, w_in, gla_w_decay, gla_b_decay, gla_out_g, sb_q_g, sb_k_g, sb_out_g,
           hg_out_g, hg_lb_logits, w_out, norm_ffn_g, w_ffn_up, w_ffn_down):
    batch, seq, d_model = x.shape
    depth = w_in.shape[0]
    x2 = x.reshape(batch * seq, d_model).astype(F32)

    gla_cols = _padded_head_cols(0, GLA_HEADS, GLA_DK, GLA_DK_PAD)
    sb_lo = GLA_HEADS * GLA_DV
    hg_lo = sb_lo + SB_HEADS * SB_DH
    w_main = _take_padded(w_in, _MAIN_SRC, 2).astype(BF16)
    w_gate = _take_padded(w_in, _GATE_SRC, 2).astype(BF16)
    wd = _take_padded(gla_w_decay.astype(F32), gla_cols, 2)
    wd = jnp.pad(wd, ((0, 0), (0, LANES - GLA_LOWRANK), (0, 0)))
    bd = _take_padded(gla_b_decay.astype(F32), gla_cols, 1)[:, None, :]
    gain_a = jnp.pad(gla_out_g.astype(F32), ((0, 0), (0, HEAD_V_PAD - GLA_DV)))[:, None, :]
    gain_c = jnp.tile(hg_out_g.astype(F32), (1, LANES // HG_DV))[:, None, :]
    w_a = _take_padded(w_out, _padded_head_cols(0, GLA_HEADS, GLA_DV, HEAD_V_PAD), 1).astype(BF16)
    w_b = w_out[:, sb_lo:hg_lo].astype(BF16)
    w_c = w_out[:, hg_lo:].astype(BF16)
    w_up = w_ffn_up.astype(BF16)
    w_down = w_ffn_down.astype(BF16)
    lb_logits = hg_lb_logits.astype(F32)
    g_mix = norm_mix_g.astype(F32)[:, None, :]
    g_ffn = norm_ffn_g.astype(F32)[:, None, :]

    for li in range(depth):
        projs = _in_proj(li, x2, g_mix, w_main, w_gate, tm=ROW_TILE, tn=MAIN_COLS // 3)
        o_a, o_c = _mixers(li, projs, wd, bd, gain_a, lb_logits, gain_c, batch=batch)
        o_b = _sb_attention(projs[0], sb_q_g[li].astype(F32), sb_k_g[li].astype(F32),
                            sb_out_g[li].astype(F32), batch=batch)
        x2 = _residual_matmul(li, x2, (o_a, o_b, o_c), (w_a, w_b, w_c), tm=ROW_TILE)
        act = _ffn_up(li, x2, g_ffn, w_up, tm=ROW_TILE, tn=w_up.shape[2] // 4)
        x2 = _residual_matmul(li, x2, (act,), (w_down,), tm=ROW_TILE // 2)
    return x2.reshape(batch, seq, d_model).astype(x.dtype)
```

```python
import functools

import numpy as np
import jax
import jax.numpy as jnp
from jax import lax
from jax.experimental import pallas as pl
from jax.experimental.pallas import tpu as pltpu

F32 = jnp.float32
BF16 = jnp.bfloat16

LANES = 128
RMS_EPS = 1e-6

GLA_HEADS, GLA_DK, GLA_DV, GLA_LOWRANK = 4, 48, 96, 16
GLA_GATE_NORMALIZER = 16.0
SB_HEADS, SB_DH, SB_BLOCK = 6, 64, 128
HG_HEADS, HG_DK, HG_DV = 4, 128, 64

GLA_DK_PAD = 64
HEAD_V_PAD = LANES
MIX_CHUNK = 128
SB_QGROUP = 4
SB_UNROLL = 2
SB_LOOP_STEPS = 1
SB_SEQS = 4
MIX_SEQS = 2
SB_DEAD_LOG2 = 150.0
LOG2_E = 1.4426950408889634
MASK_BIAS = -1e30
assert SB_QGROUP % SB_UNROLL == 0 and SB_QGROUP // SB_UNROLL >= 2
assert (SB_QGROUP // SB_UNROLL) % SB_LOOP_STEPS == 0
ROW_TILE = 1024
VMEM_LIMIT = 48 * 1024 * 1024

_MAIN_SEGS = (("gv", 512), ("gg", 512), ("hq", 512), ("hi", 256), ("hg", 256),
              ("gq", 256), ("gk", 256), ("sq", 384), ("sk", 384), ("sv", 384), ("pad", 128))
_GATE_SEGS = (("hf", 512), ("glr", 128))
_SEG = {}
for _arr, _segs in enumerate((_MAIN_SEGS, _GATE_SEGS)):
    _off = 0
    for _name, _w in _segs:
        _SEG[_name] = (_arr, _off, _w)
        _off += _w
MAIN_COLS = sum(w for _, w in _MAIN_SEGS)
GATE_COLS = sum(w for _, w in _GATE_SEGS)


def _padded_head_cols(start, heads, width, pad):
    idx = -np.ones((heads, pad), np.int64)
    idx[:, :width] = start + np.arange(heads)[:, None] * width + np.arange(width)[None, :]
    return idx.reshape(-1)


def _proj_source_columns():
    sizes = (GLA_HEADS * GLA_DK, GLA_HEADS * GLA_DK, GLA_HEADS * GLA_DV, GLA_LOWRANK,
             GLA_HEADS * GLA_DV, SB_HEADS * SB_DH, SB_HEADS * SB_DH, SB_HEADS * SB_DH,
             HG_HEADS * HG_DK, HG_HEADS * HG_DK, HG_HEADS * HG_DV, HG_HEADS * HG_DV)
    starts = np.concatenate([[0], np.cumsum(sizes)[:-1]])
    (gq, gk, gv, glr, gg, sq, sk, sv, hq, hf, hi, hg) = [int(s) for s in starts]
    src = {
        "gq": _padded_head_cols(gq, GLA_HEADS, GLA_DK, GLA_DK_PAD),
        "gk": _padded_head_cols(gk, GLA_HEADS, GLA_DK, GLA_DK_PAD),
        "gv": _padded_head_cols(gv, GLA_HEADS, GLA_DV, HEAD_V_PAD),
        "gg": _padded_head_cols(gg, GLA_HEADS, GLA_DV, HEAD_V_PAD),
        "glr": _padded_head_cols(glr, 1, GLA_LOWRANK, LANES),
        "sq": np.arange(sq, sq + SB_HEADS * SB_DH),
        "sk": np.arange(sk, sk + SB_HEADS * SB_DH),
        "sv": np.arange(sv, sv + SB_HEADS * SB_DH),
        "hq": np.arange(hq, hq + HG_HEADS * HG_DK),
        "hf": np.arange(hf, hf + HG_HEADS * HG_DK),
        "hi": np.arange(hi, hi + HG_HEADS * HG_DV),
        "hg": np.arange(hg, hg + HG_HEADS * HG_DV),
        "pad": -np.ones(LANES, np.int64),
    }
    return tuple(np.concatenate([src[name] for name, _ in segs]) for segs in (_MAIN_SEGS, _GATE_SEGS))


_MAIN_SRC, _GATE_SRC = _proj_source_columns()


def _take_padded(arr, src, axis):
    pad = src < 0
    breaks = np.flatnonzero(np.where(pad[1:] | pad[:-1], pad[1:] != pad[:-1], np.diff(src) != 1)) + 1
    pieces = []
    for run in np.split(src, breaks):
        if run[0] < 0:
            shape = list(arr.shape)
            shape[axis] = len(run)
            pieces.append(jnp.zeros(shape, arr.dtype))
        else:
            pieces.append(lax.slice_in_dim(arr, int(run[0]), int(run[-1]) + 1, axis=axis))
    return jnp.concatenate(pieces, axis=axis)


def _dot(a, b):
    return jnp.dot(a, b, preferred_element_type=F32)


def _split_bf16(x):
    hi = x.astype(BF16)
    lo = (x - hi.astype(F32)).astype(BF16)
    return hi, lo


def _softplus(z):
    return jnp.maximum(z, 0.0) + jnp.log(1.0 + jnp.exp(-jnp.abs(z)))


def _silu(z):
    return z / (1.0 + jnp.exp(-z))


def _rms_rows(x, g):
    ms = jnp.mean(x * x, axis=-1, keepdims=True)
    return x * lax.rsqrt(ms + RMS_EPS) * g


def _in_proj_kernel(x_ref, g_ref, w_ref, wg_ref, o_ref, og_ref, h_scr):
    @pl.when(pl.program_id(1) == 0)
    def _():
        h = _rms_rows(x_ref[...], g_ref[...]).astype(BF16)
        h_scr[...] = h
        og_ref[...] = _dot(h, wg_ref[...])

    o_ref[...] = _dot(h_scr[...], w_ref[...]).astype(o_ref.dtype)


def _in_proj(layer, x, g, w_main, w_gate, *, tm, tn):
    m, d = x.shape
    n, ng = w_main.shape[2], w_gate.shape[2]
    return pl.pallas_call(
        _in_proj_kernel,
        grid=(m // tm, n // tn),
        in_specs=[pl.BlockSpec((tm, d), lambda i, j: (i, 0)),
                  pl.BlockSpec((None, 1, d), lambda i, j: (layer, 0, 0)),
                  pl.BlockSpec((None, d, tn), lambda i, j: (layer, 0, j)),
                  pl.BlockSpec((None, d, ng), lambda i, j: (layer, 0, 0))],
        out_specs=[pl.BlockSpec((tm, tn), lambda i, j: (i, j)),
                   pl.BlockSpec((tm, ng), lambda i, j: (i, 0))],
        out_shape=[jax.ShapeDtypeStruct((m, n), BF16), jax.ShapeDtypeStruct((m, ng), F32)],
        scratch_shapes=[pltpu.VMEM((tm, d), BF16)],
        compiler_params=pltpu.CompilerParams(
            dimension_semantics=("arbitrary", "arbitrary"), vmem_limit_bytes=VMEM_LIMIT),
        name="in_proj",
    )(x, g, w_main, w_gate)


def _ffn_up_kernel(x_ref, g_ref, wg_ref, wu_ref, o_ref, h_scr):
    @pl.when(pl.program_id(1) == 0)
    def _():
        h_scr[...] = _rms_rows(x_ref[...], g_ref[...]).astype(BF16)

    h = h_scr[...]
    gate = _dot(h, wg_ref[...])
    up = _dot(h, wu_ref[...])
    o_ref[...] = (_silu(gate) * up).astype(o_ref.dtype)


def _ffn_up(layer, x, g, w_up, *, tm, tn):
    m, d = x.shape
    d_ff = w_up.shape[2] // 2
    nj = d_ff // tn
    return pl.pallas_call(
        _ffn_up_kernel,
        grid=(m // tm, nj),
        in_specs=[pl.BlockSpec((tm, d), lambda i, j: (i, 0)),
                  pl.BlockSpec((None, 1, d), lambda i, j: (layer, 0, 0)),
                  pl.BlockSpec((None, d, tn), lambda i, j: (layer, 0, j)),
                  pl.BlockSpec((None, d, tn), lambda i, j: (layer, 0, j + nj))],
        out_specs=pl.BlockSpec((tm, tn), lambda i, j: (i, j)),
        out_shape=jax.ShapeDtypeStruct((m, d_ff), BF16),
        scratch_shapes=[pltpu.VMEM((tm, d), BF16)],
        compiler_params=pltpu.CompilerParams(
            dimension_semantics=("arbitrary", "arbitrary"), vmem_limit_bytes=VMEM_LIMIT),
        name="ffn_up",
    )(x, g, w_up, w_up)


def _residual_matmul_kernel(n_in, res_ref, *refs):
    a_refs, w_refs, o_ref = refs[:n_in], refs[n_in:2 * n_in], refs[2 * n_in]
    acc = res_ref[...]
    for a_ref, w_ref in zip(a_refs, w_refs):
        acc = acc + _dot(a_ref[...], w_ref[...])
    o_ref[...] = acc


def _residual_matmul(layer, res, acts, weights, *, tm):
    m, n = res.shape
    n_in = len(acts)
    in_specs = [pl.BlockSpec((tm, n), lambda i: (i, 0))]
    in_specs += [pl.BlockSpec((tm, a.shape[1]), lambda i: (i, 0)) for a in acts]
    in_specs += [pl.BlockSpec((None,) + w.shape[1:], lambda i: (layer, 0, 0)) for w in weights]
    return pl.pallas_call(
        functools.partial(_residual_matmul_kernel, n_in),
        grid=(m // tm,),
        in_specs=in_specs,
        out_specs=pl.BlockSpec((tm, n), lambda i: (i, 0)),
        out_shape=jax.ShapeDtypeStruct((m, n), F32),
        compiler_params=pltpu.CompilerParams(
            dimension_semantics=("arbitrary",), vmem_limit_bytes=VMEM_LIMIT),
        name="residual_matmul",
    )(res, *acts, *weights)


def _pair_rms(x, g, lo_lanes):
    x2 = x * x
    s_lo = jnp.sum(jnp.where(lo_lanes, x2, 0.0), axis=-1, keepdims=True)
    s_hi = jnp.sum(jnp.where(lo_lanes, 0.0, x2), axis=-1, keepdims=True)
    ms = jnp.where(lo_lanes, s_lo, s_hi) * (2.0 / LANES)
    return x * lax.rsqrt(ms + RMS_EPS) * g


def _decay_sum_matrix(chunk):
    t = np.arange(chunk)
    blocks = [(t[None, :] <= t[:, None]), (t[None, :] > t[:, None])]
    h = 1
    while h < chunk:
        mid = (t // (2 * h)) * (2 * h) + h
        right = t >= mid
        m = np.where(right[:, None],
                     (t[None, :] >= mid[:, None]) & (t[None, :] <= t[:, None]),
                     (t[None, :] > t[:, None]) & (t[None, :] < mid[:, None]))
        blocks.append(m)
        h *= 2
    return np.concatenate(blocks, axis=0).astype(np.float32)


def _pair_owner_matrix(chunk):
    t = np.arange(chunk)
    x = t[:, None] ^ t[None, :]
    level = np.floor(np.log2(np.maximum(x, 1))).astype(np.int32)
    return np.where(x == 0, 0, np.where(t[None, :] < t[:, None], 1 + level, -1)).astype(np.int32)


def _chunk_step(qb, kb, lf, v_ref, gate_ref, gain, first_chunk, msum_ref, owner_ref,
                rd, wr, state_ref, o_ref, *, heads, dk_lanes, dv, dv_lanes):
    q_rd, k_rd, dec_rd = rd
    q_wr, k_wr, dec_wr = wr
    c = qb.shape[0]
    n_levels = c.bit_length() - 1
    n_tiles = n_levels + 1
    lf_b = lf.astype(BF16)

    def decay_rows(block):
        return jnp.exp(_dot(msum_ref[block * c:(block + 1) * c, :], lf_b).astype(BF16))

    def store_slot(t):
        if t == 0:
            q_wr[0] = qb
            k_wr[0] = kb.T
        elif t <= n_levels:
            e = decay_rows(1 + t)
            q_wr[t] = qb * e
            k_wr[t] = (kb * e).T
        else:
            q_wr[t] = qb * decay_rows(0)
            k_wr[t] = (kb * decay_rows(1)).T
            total = lax.dot_general(lf_b, jnp.ones((c, LANES), BF16), (((0,), (0,)), ((), ())),
                                    preferred_element_type=F32)
            dec_wr[...] = jnp.exp(total)

    owner = owner_ref[...]
    owned = [owner == t for t in range(n_tiles)]
    lane = lax.broadcasted_iota(jnp.int32, (1, LANES), 1)

    for hd in range(heads):
        g0 = (hd * dk_lanes // LANES) * LANES
        grp = slice(g0, g0 + LANES)
        if dk_lanes < LANES:
            lo = hd * dk_lanes - g0
            head_lanes = jnp.where((lane >= lo) & (lane < lo + dk_lanes), 1.0, 0.0).astype(BF16)
            pick = lambda t: q_rd[t, :, grp] * head_lanes
        else:
            pick = lambda t: q_rd[t, :, grp]
        scores = jnp.zeros((c, c), F32)
        for t in range(n_tiles):
            scores = jnp.where(owned[t], _dot(pick(t), k_rd[t, grp, :]), scores)
        v0 = (hd * dv_lanes // LANES) * LANES
        sl = slice(v0, v0 + LANES)
        v_h = v_ref[:, sl]
        state = jnp.where(first_chunk, 0.0, state_ref[hd])
        o = _dot(scores.astype(BF16), v_h) + _dot(pick(n_tiles), state.astype(BF16))
        state_ref[hd] = state * dec_rd[grp, :] + _dot(k_rd[n_tiles, grp, :], v_h)
        if dv_lanes == LANES:
            ms = jnp.sum(o * o, axis=-1, keepdims=True) * (1.0 / dv)
            y = o * lax.rsqrt(ms + RMS_EPS) * gain
        elif hd % 2 == 0:
            o_even = o
        else:
            y = _pair_rms(jnp.where(lane < dv_lanes, o_even, o), gain, lane < dv_lanes)
        if dv_lanes == LANES or hd % 2 == 1:
            o_ref[:, sl] = (y * _silu(gate_ref[:, sl].astype(F32))).astype(o_ref.dtype)

        for t in range(hd, n_tiles + 1, heads):
            store_slot(t)
        yield


def _mixer_step(scratch, step_fn):
    @pl.when(pl.program_id(1) == 0)
    def _():
        for r in scratch:
            r[...] = jnp.zeros_like(r)

    parity = lax.rem(pl.program_id(1), 2)
    for buf in (0, 1):
        @pl.when(parity == buf)
        def _():
            step_fn(1 - buf, buf)


def _mixers_kernel(layer,
                   gq_ref, gk_ref, lr_ref, wd_ref, bd_ref, hq_ref, hf_ref, lb_ref,
                   gv_ref, gg_ref, gain_a_ref, hi_ref, hg_ref, gain_c_ref, msum_ref, owner_ref,
                   oa_ref, oc_ref,
                   state_a, q_a, k_a, dec_a, state_c, q_c, k_c, dec_c):
    first_chunk = pl.program_id(1) == 1
    seqs = range(gq_ref.shape[0])

    def step(rd, wr):
        wd_hi, wd_lo = _split_bf16(wd_ref[...])
        lb_logits = lb_ref[...]
        ex = jnp.exp(lb_logits - jnp.max(lb_logits, axis=0, keepdims=True))
        probs = ex / jnp.sum(ex, axis=0, keepdims=True)
        lb = jnp.zeros_like(probs[0:1])
        for d in range(1, layer + 1):
            lb = lb + probs[d:d + 1]

        chains = []
        for sq in seqs:
            lr_hi, lr_lo = _split_bf16(lr_ref[sq])
            logits = _dot(lr_hi, wd_hi) + _dot(lr_hi, wd_lo) + _dot(lr_lo, wd_hi) + bd_ref[...]
            lf_a = -_softplus(-logits) * (1.0 / GLA_GATE_NORMALIZER)
            qb_a = (gq_ref[sq].astype(F32) * GLA_DK ** -0.5).astype(BF16)
            chains.append(_chunk_step(
                qb_a, gk_ref[sq], lf_a, gv_ref.at[sq], gg_ref.at[sq], gain_a_ref[...], first_chunk,
                msum_ref, owner_ref, (q_a.at[rd].at[sq], k_a.at[rd].at[sq], dec_a.at[rd].at[sq]),
                (q_a.at[wr].at[sq], k_a.at[wr].at[sq], dec_a.at[wr].at[sq]), state_a.at[sq],
                oa_ref.at[sq], heads=GLA_HEADS, dk_lanes=GLA_DK_PAD, dv=GLA_DV, dv_lanes=HEAD_V_PAD))

            hf = hf_ref[sq]
            sp = _softplus(-hf)
            log_sig = -sp
            a = jnp.log(jnp.maximum(lb, 1e-30))
            b = jnp.log(1.0 - lb) + log_sig
            lae = jnp.maximum(a, b) + jnp.log(1.0 + jnp.exp(-jnp.abs(a - b)))
            lf_c = jnp.where(lb > 0.0, lae, log_sig)
            kb_c = ((1.0 - lb) * jnp.exp(-(hf + sp))).astype(BF16)
            chains.append(_chunk_step(
                hq_ref[sq], kb_c, lf_c, hi_ref.at[sq], hg_ref.at[sq], gain_c_ref[...], first_chunk,
                msum_ref, owner_ref, (q_c.at[rd].at[sq], k_c.at[rd].at[sq], dec_c.at[rd].at[sq]),
                (q_c.at[wr].at[sq], k_c.at[wr].at[sq], dec_c.at[wr].at[sq]), state_c.at[sq],
                oc_ref.at[sq], heads=HG_HEADS, dk_lanes=HG_DK, dv=HG_DV, dv_lanes=HG_DV))
        for _ in zip(*chains):
            pass

    _mixer_step((state_a, q_a, k_a, dec_a, state_c, q_c, k_c, dec_c), step)


def _mixers(layer, projs, wd, bd, gain_a, lb_logits, gain_c, *, batch):
    assert GLA_HEADS == HG_HEADS
    m = projs[0].shape[0]
    t = m // batch
    n_chunks = t // MIX_CHUNK
    ns = MIX_SEQS
    msum = jnp.asarray(_decay_sum_matrix(MIX_CHUNK), BF16)
    owner = jnp.asarray(_pair_owner_matrix(MIX_CHUNK))
    lead = lambda s: jnp.minimum(s, n_chunks - 1)
    lag = lambda s: jnp.maximum(s - 1, 0)
    projs3 = [p.reshape(batch, t, p.shape[1]) for p in projs]

    def seg(name, row_of):
        arr, off, width = _SEG[name]
        assert off % width == 0
        spec = pl.BlockSpec((ns, MIX_CHUNK, width), lambda g, s: (g, row_of(s), off // width))
        return projs3[arr], spec

    const = lambda e: (e, pl.BlockSpec(e.shape, lambda g, s: (0, 0)))
    per_layer = lambda e: (e, pl.BlockSpec((None,) + e.shape[1:], lambda g, s: (layer, 0, 0)))
    operands = [seg("gq", lead), seg("gk", lead), seg("glr", lead), per_layer(wd), per_layer(bd),
                seg("hq", lead), seg("hf", lead), const(lb_logits),
                seg("gv", lag), seg("gg", lag), per_layer(gain_a),
                seg("hi", lag), seg("hg", lag), per_layer(gain_c), const(msum), const(owner)]
    widths = (GLA_HEADS * HEAD_V_PAD, HG_HEADS * HG_DV)
    n_slots = MIX_CHUNK.bit_length() + 1

    def scratch(dk_lanes):
        w = GLA_HEADS * dk_lanes
        return [pltpu.VMEM((ns, GLA_HEADS, LANES, HEAD_V_PAD), F32),
                pltpu.VMEM((2, ns, n_slots, MIX_CHUNK, w), BF16),
                pltpu.VMEM((2, ns, n_slots, w, MIX_CHUNK), BF16),
                pltpu.VMEM((2, ns, w, LANES), F32)]

    out_spec = lambda width: pl.BlockSpec((ns, MIX_CHUNK, width), lambda g, s: (g, lag(s), 0))
    outs = pl.pallas_call(
        functools.partial(_mixers_kernel, layer),
        grid=(batch // ns, n_chunks + 1),
        in_specs=[spec for _, spec in operands],
        out_specs=[out_spec(w) for w in widths],
        out_shape=[jax.ShapeDtypeStruct((batch, t, w), BF16) for w in widths],
        scratch_shapes=scratch(GLA_DK_PAD) + scratch(HG_DK),
        compiler_params=pltpu.CompilerParams(
            dimension_semantics=("arbitrary", "arbitrary"), vmem_limit_bytes=VMEM_LIMIT),
        name="mixers",
    )(*[arr for arr, _ in operands])
    return [o.reshape(m, o.shape[2]) for o in outs]


def _sb_kernel(q_ref, k_ref, v_ref, qg_ref, kg_ref, og_ref, ut_ref, o_ref,
               kn_scr, vt_scr, q2t_scr, c_scr, acc_scr,
               bias_scr, sp_scr, zs_scr, sp0_scr, w_scr):
    grp = pl.program_id(2)
    bk = SB_BLOCK
    seqs = range(q_ref.shape[0])
    gq = q_ref.shape[1]
    n_kb = k_ref.shape[1] // bk
    lo_lanes = lax.broadcasted_iota(jnp.int32, (1, LANES), 1) < SB_DH

    @pl.when(grp == 0)
    def _():
        for sq in seqs:
            kn_scr[sq] = _pair_rms(k_ref[sq].astype(F32), kg_ref[...], lo_lanes).astype(BF16)

        def transpose_block(kb, carry):
            rows = pl.ds(pl.multiple_of(kb * bk, bk), bk)
            for sq in seqs:
                vt_scr[sq, kb] = v_ref[sq, rows, :].astype(F32).T.astype(BF16)
            return carry

        lax.fori_loop(0, n_kb, transpose_block, 0)

        key = lax.broadcasted_iota(jnp.int32, (bk, bk), 0)
        qry = lax.broadcasted_iota(jnp.int32, (bk, bk), 1)
        bias_scr[...] = jnp.where(key < qry, 0.0, MASK_BIAS)

    for sq in seqs:
        qn = _pair_rms(q_ref[sq].astype(F32), qg_ref[...], lo_lanes) * (SB_DH ** -0.5 * LOG2_E)
        q2t_scr[sq, :, :gq] = jnp.where(lo_lanes, qn, 0.0).T.astype(BF16)
        q2t_scr[sq, :, gq:] = jnp.where(lo_lanes, 0.0, qn).T.astype(BF16)
    acc_scr[...] = jnp.zeros_like(acc_scr)
    c_scr[...] = jnp.zeros_like(c_scr)

    last = grp * SB_QGROUP + SB_QGROUP - 1

    def score_columns(sq, kt, u, cols, diagonal):
        z = _dot(kt, q2t_scr[sq, :, cols])
        if diagonal:
            parts = [z[:, :bk] + bias_scr[...]] + ([z[:, bk:]] if z.shape[1] > bk else [])
            z = jnp.concatenate(parts, axis=1)
        sp = jnp.maximum(z, 0.0) + jnp.log2(1.0 + jnp.exp2(-jnp.abs(z)))
        sp_scr[sq, u, :, cols] = sp.astype(BF16)
        zs_scr[sq, u, :, cols] = z - sp
        sp0_scr[sq, u, :, cols] = sp[0:8]

    def stage_scores(p, masked):
        for u in range(SB_UNROLL):
            m = p * SB_UNROLL + u
            for sq in seqs:
                kt = kn_scr[sq, pl.ds(pl.multiple_of((last - m) * bk, bk), bk), :]
                if not masked:
                    score_columns(sq, kt, u, slice(0, 2 * gq), False)
                    continue
                lo = (SB_QGROUP - 1 - m) * bk
                for half in (0, gq):
                    score_columns(sq, kt, u, slice(half + lo, half + gq), True)
                    if lo:
                        hidden = slice(half, half + lo)
                        sp_scr[sq, u, :, hidden] = jnp.zeros((bk, lo), BF16)
                        zs_scr[sq, u, :, hidden] = jnp.full((bk, lo), MASK_BIAS, F32)
                        sp0_scr[sq, u, :, hidden] = jnp.zeros((8, lo), F32)

    def stage_weights(p):
        c = [c_scr[sq] for sq in seqs]
        for u in range(SB_UNROLL):
            for sq in seqs:
                later = _dot(ut_ref[...], sp_scr[sq, u])
                w_scr[sq, u] = jnp.exp2(zs_scr[sq, u] - later - c[sq]).astype(BF16)
                c[sq] = c[sq] + later[0:1] + sp0_scr[sq, u, 0:1, :]
        for sq in seqs:
            c_scr[sq] = c[sq]

    def stage_values(p):
        for u in range(SB_UNROLL):
            for sq in seqs:
                vt = vt_scr[sq, last - (p * SB_UNROLL + u)]
                w = w_scr[sq, u]
                acc_scr[sq, 0:SB_DH, :] += _dot(vt[0:SB_DH], w[:, :gq])
                acc_scr[sq, SB_DH:, :] += _dot(vt[SB_DH:], w[:, gq:])

    n_masked = SB_QGROUP // SB_UNROLL
    for p in range(n_masked):
        if p >= 2:
            stage_values(p - 2)
        if p >= 1:
            stage_weights(p - 1)
        stage_scores(p, True)

    n_trips = grp * (n_masked // SB_LOOP_STEPS)

    def keep_going(carry):
        trip, c_min = carry
        return jnp.logical_and(trip < n_trips, c_min < SB_DEAD_LOG2)

    def body(carry):
        trip, _ = carry
        for s in range(SB_LOOP_STEPS):
            p = n_masked + trip * SB_LOOP_STEPS + s
            stage_values(p - 2)
            stage_weights(p - 1)
            stage_scores(p, False)
        return trip + 1, jnp.min(c_scr[...])

    trips, c_min = lax.while_loop(keep_going, body, (jnp.int32(0), jnp.float32(0.0)))
    p_end = n_masked + trips * SB_LOOP_STEPS
    stage_values(p_end - 2)

    @pl.when(c_min < SB_DEAD_LOG2)
    def _():
        stage_weights(p_end - 1)
        stage_values(p_end - 1)

    for sq in seqs:
        o_ref[sq] = _pair_rms(acc_scr[sq].T, og_ref[...], lo_lanes).astype(o_ref.dtype)


def _sb_attention(proj, qg, kg, og, *, batch):
    m, cols = proj.shape
    t = m // batch
    bk = SB_BLOCK
    gq = SB_QGROUP * bk
    n_groups = t // gq
    n_pairs = SB_HEADS * SB_DH // LANES
    ns = SB_SEQS
    oq, ok, ov = (_SEG[s][1] // LANES for s in ("sq", "sk", "sv"))
    j = np.arange(bk)
    ut = jnp.asarray((j[None, :] > j[:, None]).astype(np.float32), BF16)
    pair = lambda g: jnp.tile(g, 2).reshape(1, LANES)
    const = lambda shape: pl.BlockSpec(shape, lambda b, p, i: (0, 0))
    proj3 = proj.reshape(batch, t, cols)
    out = pl.pallas_call(
        _sb_kernel,
        grid=(batch // ns, n_pairs, n_groups),
        in_specs=[pl.BlockSpec((ns, gq, LANES), lambda b, p, i: (b, i, oq + p)),
                  pl.BlockSpec((ns, t, LANES), lambda b, p, i: (b, 0, ok + p)),
                  pl.BlockSpec((ns, t, LANES), lambda b, p, i: (b, 0, ov + p)),
                  const((1, LANES)), const((1, LANES)), const((1, LANES)),
                  const((bk, bk))],
        out_specs=pl.BlockSpec((ns, gq, LANES), lambda b, p, i: (b, i, p)),
        out_shape=jax.ShapeDtypeStruct((batch, t, n_pairs * LANES), BF16),
        scratch_shapes=[pltpu.VMEM((ns, t, LANES), BF16),
                        pltpu.VMEM((ns, t // bk, LANES, bk), BF16),
                        pltpu.VMEM((ns, LANES, 2 * gq), BF16),
                        pltpu.VMEM((ns, 1, 2 * gq), F32),
                        pltpu.VMEM((ns, LANES, gq), F32),
                        pltpu.VMEM((bk, bk), F32),
                        pltpu.VMEM((ns, SB_UNROLL, bk, 2 * gq), BF16),
                        pltpu.VMEM((ns, SB_UNROLL, bk, 2 * gq), F32),
                        pltpu.VMEM((ns, SB_UNROLL, 8, 2 * gq), F32),
                        pltpu.VMEM((ns, SB_UNROLL, bk, 2 * gq), BF16)],
        compiler_params=pltpu.CompilerParams(
            dimension_semantics=("arbitrary", "arbitrary", "arbitrary"),
            vmem_limit_bytes=VMEM_LIMIT),
        name="sb_attention",
    )(proj3, proj3, proj3, pair(qg), pair(kg), pair(og), ut)
    return out.reshape(m, n_pairs * LANES)


def kernel(x, norm_mix_g, w_in, gla_w_decay, gla_b_decay, gla_out_g, sb_q_g, sb_k_g, sb_out_g,
           hg_out_g, hg_lb_logits, w_out, norm_ffn_g, w_ffn_up, w_ffn_down):
    batch, seq, d_model = x.shape
    depth = w_in.shape[0]
    x2 = x.reshape(batch * seq, d_model).astype(F32)

    gla_cols = _padded_head_cols(0, GLA_HEADS, GLA_DK, GLA_DK_PAD)
    sb_lo = GLA_HEADS * GLA_DV
    hg_lo = sb_lo + SB_HEADS * SB_DH
    w_main = _take_padded(w_in, _MAIN_SRC, 2).astype(BF16)
    w_gate = _take_padded(w_in, _GATE_SRC, 2).astype(BF16)
    wd = _take_padded(gla_w_decay.astype(F32), gla_cols, 2)
    wd = jnp.pad(wd, ((0, 0), (0, LANES - GLA_LOWRANK), (0, 0)))
    bd = _take_padded(gla_b_decay.astype(F32), gla_cols, 1)[:, None, :]
    gain_a = jnp.pad(gla_out_g.astype(F32), ((0, 0), (0, HEAD_V_PAD - GLA_DV)))[:, None, :]
    gain_c = jnp.tile(hg_out_g.astype(F32), (1, LANES // HG_DV))[:, None, :]
    w_a = _take_padded(w_out, _padded_head_cols(0, GLA_HEADS, GLA_DV, HEAD_V_PAD), 1).astype(BF16)
    w_b = w_out[:, sb_lo:hg_lo].astype(BF16)
    w_c = w_out[:, hg_lo:].astype(BF16)
    w_up = w_ffn_up.astype(BF16)
    w_down = w_ffn_down.astype(BF16)
    lb_logits = hg_lb_logits.astype(F32)
    g_mix = norm_mix_g.astype(F32)[:, None, :]
    g_ffn = norm_ffn_g.astype(F32)[:, None, :]

    for li in range(depth):
        projs = _in_proj(li, x2, g_mix, w_main, w_gate, tm=ROW_TILE, tn=MAIN_COLS // 3)
        o_a, o_c = _mixers(li, projs, wd, bd, gain_a, lb_logits, gain_c, batch=batch)
        o_b = _sb_attention(projs[0], sb_q_g[li].astype(F32), sb_k_g[li].astype(F32),
                            sb_out_g[li].astype(F32), batch=batch)
        x2 = _residual_matmul(li, x2, (o_a, o_b, o_c), (w_a, w_b, w_c), tm=ROW_TILE)
        act = _ffn_up(li, x2, g_ffn, w_up, tm=ROW_TILE, tn=w_up.shape[2] // 4)
        x2 = _residual_matmul(li, x2, (act,), (w_down,), tm=ROW_TILE // 2)
    return x2.reshape(batch, seq, d_model).astype(x.dtype)
```

```python
import functools

import numpy as np
import jax
import jax.numpy as jnp
from jax import lax
from jax.experimental import pallas as pl
from jax.experimental.pallas import tpu as pltpu

F32 = jnp.float32
BF16 = jnp.bfloat16

LANES = 128
RMS_EPS = 1e-6

GLA_HEADS, GLA_DK, GLA_DV, GLA_LOWRANK = 4, 48, 96, 16
GLA_GATE_NORMALIZER = 16.0
SB_HEADS, SB_DH, SB_BLOCK = 6, 64, 128
HG_HEADS, HG_DK, HG_DV = 4, 128, 64

GLA_DK_PAD = 64
HEAD_V_PAD = LANES
MIX_CHUNK = 128
SB_QGROUP = 4
SB_UNROLL = 2
SB_LOOP_STEPS = 1
SB_SEQS = 4
MIX_SEQS = 2
SB_DEAD_LOG2 = 150.0
LOG2_E = 1.4426950408889634
MASK_BIAS = -1e30
assert SB_QGROUP % SB_UNROLL == 0 and SB_QGROUP // SB_UNROLL >= 2
assert (SB_QGROUP // SB_UNROLL) % SB_LOOP_STEPS == 0
ROW_TILE = 1024
VMEM_LIMIT = 48 * 1024 * 1024

_MAIN_SEGS = (("gv", 512), ("gg", 512), ("hq", 512), ("hi", 256), ("hg", 256),
              ("gq", 256), ("gk", 256), ("sq", 384), ("sk", 384), ("sv", 384), ("pad", 128))
_GATE_SEGS = (("hf", 512), ("glr", 128))
_SEG = {}
for _arr, _segs in enumerate((_MAIN_SEGS, _GATE_SEGS)):
    _off = 0
    for _name, _w in _segs:
        _SEG[_name] = (_arr, _off, _w)
        _off += _w
MAIN_COLS = sum(w for _, w in _MAIN_SEGS)
GATE_COLS = sum(w for _, w in _GATE_SEGS)


def _padded_head_cols(start, heads, width, pad):
    idx = -np.ones((heads, pad), np.int64)
    idx[:, :width] = start + np.arange(heads)[:, None] * width + np.arange(width)[None, :]
    return idx.reshape(-1)


def _proj_source_columns():
    sizes = (GLA_HEADS * GLA_DK, GLA_HEADS * GLA_DK, GLA_HEADS * GLA_DV, GLA_LOWRANK,
             GLA_HEADS * GLA_DV, SB_HEADS * SB_DH, SB_HEADS * SB_DH, SB_HEADS * SB_DH,
             HG_HEADS * HG_DK, HG_HEADS * HG_DK, HG_HEADS * HG_DV, HG_HEADS * HG_DV)
    starts = np.concatenate([[0], np.cumsum(sizes)[:-1]])
    (gq, gk, gv, glr, gg, sq, sk, sv, hq, hf, hi, hg) = [int(s) for s in starts]
    src = {
        "gq": _padded_head_cols(gq, GLA_HEADS, GLA_DK, GLA_DK_PAD),
        "gk": _padded_head_cols(gk, GLA_HEADS, GLA_DK, GLA_DK_PAD),
        "gv": _padded_head_cols(gv, GLA_HEADS, GLA_DV, HEAD_V_PAD),
        "gg": _padded_head_cols(gg, GLA_HEADS, GLA_DV, HEAD_V_PAD),
        "glr": _padded_head_cols(glr, 1, GLA_LOWRANK, LANES),
        "sq": np.arange(sq, sq + SB_HEADS * SB_DH),
        "sk": np.arange(sk, sk + SB_HEADS * SB_DH),
        "sv": np.arange(sv, sv + SB_HEADS * SB_DH),
        "hq": np.arange(hq, hq + HG_HEADS * HG_DK),
        "hf": np.arange(hf, hf + HG_HEADS * HG_DK),
        "hi": np.arange(hi, hi + HG_HEADS * HG_DV),
        "hg": np.arange(hg, hg + HG_HEADS * HG_DV),
        "pad": -np.ones(LANES, np.int64),
    }
    return tuple(np.concatenate([src[name] for name, _ in segs]) for segs in (_MAIN_SEGS, _GATE_SEGS))


_MAIN_SRC, _GATE_SRC = _proj_source_columns()


def _take_padded(arr, src, axis):
    pad = src < 0
    breaks = np.flatnonzero(np.where(pad[1:] | pad[:-1], pad[1:] != pad[:-1], np.diff(src) != 1)) + 1
    pieces = []
    for run in np.split(src, breaks):
        if run[0] < 0:
            shape = list(arr.shape)
            shape[axis] = len(run)
            pieces.append(jnp.zeros(shape, arr.dtype))
        else:
            pieces.append(lax.slice_in_dim(arr, int(run[0]), int(run[-1]) + 1, axis=axis))
    return jnp.concatenate(pieces, axis=axis)


def _dot(a, b):
    return jnp.dot(a, b, preferred_element_type=F32)


def _split_bf16(x):
    hi = x.astype(BF16)
    lo = (x - hi.astype(F32)).astype(BF16)
    return hi, lo


def _softplus(z):
    return jnp.maximum(z, 0.0) + jnp.log(1.0 + jnp.exp(-jnp.abs(z)))


def _silu(z):
    return z / (1.0 + jnp.exp(-z))


def _rms_rows(x, g):
    ms = jnp.mean(x * x, axis=-1, keepdims=True)
    return x * lax.rsqrt(ms + RMS_EPS) * g


def _in_proj_kernel(x_ref, g_ref, w_ref, wg_ref, o_ref, og_ref, h_scr):
    @pl.when(pl.program_id(1) == 0)
    def _():
        h = _rms_rows(x_ref[...], g_ref[...]).astype(BF16)
        h_scr[...] = h
        og_ref[...] = _dot(h, wg_ref[...])

    o_ref[...] = _dot(h_scr[...], w_ref[...]).astype(o_ref.dtype)


def _in_proj(layer, x, g, w_main, w_gate, *, tm, tn):
    m, d = x.shape
    n, ng = w_main.shape[2], w_gate.shape[2]
    return pl.pallas_call(
        _in_proj_kernel,
        grid=(m // tm, n // tn),
        in_specs=[pl.BlockSpec((tm, d), lambda i, j: (i, 0)),
                  pl.BlockSpec((None, 1, d), lambda i, j: (layer, 0, 0)),
                  pl.BlockSpec((None, d, tn), lambda i, j: (layer, 0, j)),
                  pl.BlockSpec((None, d, ng), lambda i, j: (layer, 0, 0))],
        out_specs=[pl.BlockSpec((tm, tn), lambda i, j: (i, j)),
                   pl.BlockSpec((tm, ng), lambda i, j: (i, 0))],
        out_shape=[jax.ShapeDtypeStruct((m, n), BF16), jax.ShapeDtypeStruct((m, ng), F32)],
        scratch_shapes=[pltpu.VMEM((tm, d), BF16)],
        compiler_params=pltpu.CompilerParams(
            dimension_semantics=("arbitrary", "arbitrary"), vmem_limit_bytes=VMEM_LIMIT),
        name="in_proj",
    )(x, g, w_main, w_gate)


def _ffn_up_kernel(x_ref, g_ref, wg_ref, wu_ref, o_ref, h_scr):
    @pl.when(pl.program_id(1) == 0)
    def _():
        h_scr[...] = _rms_rows(x_ref[...], g_ref[...]).astype(BF16)

    h = h_scr[...]
    gate = _dot(h, wg_ref[...])
    up = _dot(h, wu_ref[...])
    o_ref[...] = (_silu(gate) * up).astype(o_ref.dtype)


def _ffn_up(layer, x, g, w_up, *, tm, tn):
    m, d = x.shape
    d_ff = w_up.shape[2] // 2
    nj = d_ff // tn
    return pl.pallas_call(
        _ffn_up_kernel,
        grid=(m // tm, nj),
        in_specs=[pl.BlockSpec((tm, d), lambda i, j: (i, 0)),
                  pl.BlockSpec((None, 1, d), lambda i, j: (layer, 0, 0)),
                  pl.BlockSpec((None, d, tn), lambda i, j: (layer, 0, j)),
                  pl.BlockSpec((None, d, tn), lambda i, j: (layer, 0, j + nj))],
        out_specs=pl.BlockSpec((tm, tn), lambda i, j: (i, j)),
        out_shape=jax.ShapeDtypeStruct((m, d_ff), BF16),
        scratch_shapes=[pltpu.VMEM((tm, d), BF16)],
        compiler_params=pltpu.CompilerParams(
            dimension_semantics=("arbitrary", "arbitrary"), vmem_limit_bytes=VMEM_LIMIT),
        name="ffn_up",
    )(x, g, w_up, w_up)


def _residual_matmul_kernel(n_in, res_ref, *refs):
    a_refs, w_refs, o_ref = refs[:n_in], refs[n_in:2 * n_in], refs[2 * n_in]
    acc = res_ref[...]
    for a_ref, w_ref in zip(a_refs, w_refs):
        acc = acc + _dot(a_ref[...], w_ref[...])
    o_ref[...] = acc


def _residual_matmul(layer, res, acts, weights, *, tm):
    m, n = res.shape
    n_in = len(acts)
    in_specs = [pl.BlockSpec((tm, n), lambda i: (i, 0))]
    in_specs += [pl.BlockSpec((tm, a.shape[1]), lambda i: (i, 0)) for a in acts]
    in_specs += [pl.BlockSpec((None,) + w.shape[1:], lambda i: (layer, 0, 0)) for w in weights]
    return pl.pallas_call(
        functools.partial(_residual_matmul_kernel, n_in),
        grid=(m // tm,),
        in_specs=in_specs,
        out_specs=pl.BlockSpec((tm, n), lambda i: (i, 0)),
        out_shape=jax.ShapeDtypeStruct((m, n), F32),
        compiler_params=pltpu.CompilerParams(
            dimension_semantics=("arbitrary",), vmem_limit_bytes=VMEM_LIMIT),
        name="residual_matmul",
    )(res, *acts, *weights)


def _pair_rms(x, g, lo_lanes):
    x2 = x * x
    s_lo = jnp.sum(jnp.where(lo_lanes, x2, 0.0), axis=-1, keepdims=True)
    s_hi = jnp.sum(jnp.where(lo_lanes, 0.0, x2), axis=-1, keepdims=True)
    ms = jnp.where(lo_lanes, s_lo, s_hi) * (2.0 / LANES)
    return x * lax.rsqrt(ms + RMS_EPS) * g


def _decay_sum_matrix(chunk):
    t = np.arange(chunk)
    blocks = [(t[None, :] <= t[:, None])]
    h = 1
    while h < chunk:
        mid = (t // (2 * h)) * (2 * h) + h
        right = t >= mid
        m = np.where(right[:, None],
                     (t[None, :] >= mid[:, None]) & (t[None, :] <= t[:, None]),
                     (t[None, :] > t[:, None]) & (t[None, :] < mid[:, None]))
        blocks.append(m)
        h *= 2
    return np.concatenate(blocks, axis=0).astype(np.float32)


def _pair_owner_matrix(chunk):
    t = np.arange(chunk)
    x = t[:, None] ^ t[None, :]
    level = np.floor(np.log2(np.maximum(x, 1))).astype(np.int32)
    return np.where(x == 0, 0, np.where(t[None, :] < t[:, None], 1 + level, -1)).astype(np.int32)


def _chunk_step(qb, kb, lf, v_ref, gate_ref, gain, first_chunk, msum_ref, owner_ref,
                rd, wr, state_ref, o_ref, *, heads, dk_lanes, dv, dv_lanes):
    q_rd, k_rd, dec_rd = rd
    q_wr, k_wr, dec_wr = wr
    c = qb.shape[0]
    n_levels = c.bit_length() - 1
    n_tiles = n_levels + 1
    lf_b = lf.astype(BF16)

    def decay_rows(block):
        return jnp.exp(_dot(msum_ref[block * c:(block + 1) * c, :], lf_b).astype(BF16))

    def store_slot(t):
        if t == 0:
            q_wr[0] = qb
            k_wr[0] = kb.T
        elif t <= n_levels:
            e = decay_rows(t)
            q_wr[t] = qb * e
            k_wr[t] = (kb * e).T
        else:
            prefix = _dot(msum_ref[0:c, :], lf_b)
            total = prefix[c - 1:c, :]
            q_wr[t] = qb * jnp.exp(prefix.astype(BF16))
            k_wr[t] = (kb * jnp.exp((total - prefix).astype(BF16))).T
            dec_wr[...] = jnp.exp(jnp.broadcast_to(total, (LANES, total.shape[1])).T)

    owner = owner_ref[...]
    owned = [owner == t for t in range(n_tiles)]
    lane = lax.broadcasted_iota(jnp.int32, (1, LANES), 1)

    for hd in range(heads):
        g0 = (hd * dk_lanes // LANES) * LANES
        grp = slice(g0, g0 + LANES)
        if dk_lanes < LANES:
            lo = hd * dk_lanes - g0
            head_lanes = jnp.where((lane >= lo) & (lane < lo + dk_lanes), 1.0, 0.0).astype(BF16)
            pick = lambda t: q_rd[t, :, grp] * head_lanes
        else:
            pick = lambda t: q_rd[t, :, grp]
        scores = jnp.zeros((c, c), F32)
        for t in range(n_tiles):
            scores = jnp.where(owned[t], _dot(pick(t), k_rd[t, grp, :]), scores)
        v0 = (hd * dv_lanes // LANES) * LANES
        sl = slice(v0, v0 + LANES)
        v_h = v_ref[:, sl]
        state = jnp.where(first_chunk, 0.0, state_ref[hd])
        o = _dot(scores.astype(BF16), v_h) + _dot(pick(n_tiles), state.astype(BF16))
        state_ref[hd] = state * dec_rd[grp, :] + _dot(k_rd[n_tiles, grp, :], v_h)
        if dv_lanes == LANES:
            ms = jnp.sum(o * o, axis=-1, keepdims=True) * (1.0 / dv)
            y = o * lax.rsqrt(ms + RMS_EPS) * gain
        elif hd % 2 == 0:
            o_even = o
        else:
            y = _pair_rms(jnp.where(lane < dv_lanes, o_even, o), gain, lane < dv_lanes)
        if dv_lanes == LANES or hd % 2 == 1:
            o_ref[:, sl] = (y * _silu(gate_ref[:, sl].astype(F32))).astype(o_ref.dtype)

        for t in range(hd, n_tiles + 1, heads):
            store_slot(t)
        yield


def _mixer_step(scratch, step_fn):
    @pl.when(pl.program_id(1) == 0)
    def _():
        for r in scratch:
            r[...] = jnp.zeros_like(r)

    parity = lax.rem(pl.program_id(1), 2)
    for buf in (0, 1):
        @pl.when(parity == buf)
        def _():
            step_fn(1 - buf, buf)


def _mixers_kernel(layer,
                   gq_ref, gk_ref, lr_ref, wd_ref, bd_ref, hq_ref, hf_ref, lb_ref,
                   gv_ref, gg_ref, gain_a_ref, hi_ref, hg_ref, gain_c_ref, msum_ref, owner_ref,
                   oa_ref, oc_ref,
                   state_a, q_a, k_a, dec_a, state_c, q_c, k_c, dec_c):
    first_chunk = pl.program_id(1) == 1
    seqs = range(gq_ref.shape[0])

    def step(rd, wr):
        wd_hi, wd_lo = _split_bf16(wd_ref[...])
        lb_logits = lb_ref[...]
        ex = jnp.exp(lb_logits - jnp.max(lb_logits, axis=0, keepdims=True))
        probs = ex / jnp.sum(ex, axis=0, keepdims=True)
        lb = jnp.zeros_like(probs[0:1])
        for d in range(1, layer + 1):
            lb = lb + probs[d:d + 1]

        chains = []
        for sq in seqs:
            lr_hi, lr_lo = _split_bf16(lr_ref[sq])
            logits = _dot(lr_hi, wd_hi) + _dot(lr_hi, wd_lo) + _dot(lr_lo, wd_hi) + bd_ref[...]
            lf_a = -_softplus(-logits) * (1.0 / GLA_GATE_NORMALIZER)
            qb_a = (gq_ref[sq].astype(F32) * GLA_DK ** -0.5).astype(BF16)
            chains.append(_chunk_step(
                qb_a, gk_ref[sq], lf_a, gv_ref.at[sq], gg_ref.at[sq], gain_a_ref[...], first_chunk,
                msum_ref, owner_ref, (q_a.at[rd].at[sq], k_a.at[rd].at[sq], dec_a.at[rd].at[sq]),
                (q_a.at[wr].at[sq], k_a.at[wr].at[sq], dec_a.at[wr].at[sq]), state_a.at[sq],
                oa_ref.at[sq], heads=GLA_HEADS, dk_lanes=GLA_DK_PAD, dv=GLA_DV, dv_lanes=HEAD_V_PAD))

            hf = hf_ref[sq]
            sp = _softplus(-hf)
            log_sig = -sp
            a = jnp.log(jnp.maximum(lb, 1e-30))
            b = jnp.log(1.0 - lb) + log_sig
            lae = jnp.maximum(a, b) + jnp.log(1.0 + jnp.exp(-jnp.abs(a - b)))
            lf_c = jnp.where(lb > 0.0, lae, log_sig)
            kb_c = ((1.0 - lb) * jnp.exp(-(hf + sp))).astype(BF16)
            chains.append(_chunk_step(
                hq_ref[sq], kb_c, lf_c, hi_ref.at[sq], hg_ref.at[sq], gain_c_ref[...], first_chunk,
                msum_ref, owner_ref, (q_c.at[rd].at[sq], k_c.at[rd].at[sq], dec_c.at[rd].at[sq]),
                (q_c.at[wr].at[sq], k_c.at[wr].at[sq], dec_c.at[wr].at[sq]), state_c.at[sq],
                oc_ref.at[sq], heads=HG_HEADS, dk_lanes=HG_DK, dv=HG_DV, dv_lanes=HG_DV))
        for _ in zip(*chains):
            pass

    _mixer_step((state_a, q_a, k_a, dec_a, state_c, q_c, k_c, dec_c), step)


def _mixers(layer, projs, wd, bd, gain_a, lb_logits, gain_c, *, batch):
    assert GLA_HEADS == HG_HEADS
    m = projs[0].shape[0]
    t = m // batch
    n_chunks = t // MIX_CHUNK
    ns = MIX_SEQS
    msum = jnp.asarray(_decay_sum_matrix(MIX_CHUNK), BF16)
    owner = jnp.asarray(_pair_owner_matrix(MIX_CHUNK))
    lead = lambda s: jnp.minimum(s, n_chunks - 1)
    lag = lambda s: jnp.maximum(s - 1, 0)
    projs3 = [p.reshape(batch, t, p.shape[1]) for p in projs]

    def seg(name, row_of):
        arr, off, width = _SEG[name]
        assert off % width == 0
        spec = pl.BlockSpec((ns, MIX_CHUNK, width), lambda g, s: (g, row_of(s), off // width))
        return projs3[arr], spec

    const = lambda e: (e, pl.BlockSpec(e.shape, lambda g, s: (0, 0)))
    per_layer = lambda e: (e, pl.BlockSpec((None,) + e.shape[1:], lambda g, s: (layer, 0, 0)))
    operands = [seg("gq", lead), seg("gk", lead), seg("glr", lead), per_layer(wd), per_layer(bd),
                seg("hq", lead), seg("hf", lead), const(lb_logits),
                seg("gv", lag), seg("gg", lag), per_layer(gain_a),
                seg("hi", lag), seg("hg", lag), per_layer(gain_c), const(msum), const(owner)]
    widths = (GLA_HEADS * HEAD_V_PAD, HG_HEADS * HG_DV)
    n_slots = MIX_CHUNK.bit_length() + 1

    def scratch(dk_lanes):
        w = GLA_HEADS * dk_lanes
        return [pltpu.VMEM((ns, GLA_HEADS, LANES, HEAD_V_PAD), F32),
                pltpu.VMEM((2, ns, n_slots, MIX_CHUNK, w), BF16),
                pltpu.VMEM((2, ns, n_slots, w, MIX_CHUNK), BF16),
                pltpu.VMEM((2, ns, w, LANES), F32)]

    out_spec = lambda width: pl.BlockSpec((ns, MIX_CHUNK, width), lambda g, s: (g, lag(s), 0))
    outs = pl.pallas_call(
        functools.partial(_mixers_kernel, layer),
        grid=(batch // ns, n_chunks + 1),
        in_specs=[spec for _, spec in operands],
        out_specs=[out_spec(w) for w in widths],
        out_shape=[jax.ShapeDtypeStruct((batch, t, w), BF16) for w in widths],
        scratch_shapes=scratch(GLA_DK_PAD) + scratch(HG_DK),
        compiler_params=pltpu.CompilerParams(
            dimension_semantics=("arbitrary", "arbitrary"), vmem_limit_bytes=VMEM_LIMIT),
        name="mixers",
    )(*[arr for arr, _ in operands])
    return [o.reshape(m, o.shape[2]) for o in outs]


def _sb_kernel(q_ref, k_ref, v_ref, qg_ref, kg_ref, og_ref, ut_ref, o_ref,
               kn_scr, vt_scr, q2t_scr, c_scr, acc_scr,
               bias_scr, sp_scr, zs_scr, sp0_scr, w_scr):
    grp = pl.program_id(2)
    bk = SB_BLOCK
    seqs = range(q_ref.shape[0])
    gq = q_ref.shape[1]
    n_kb = k_ref.shape[1] // bk
    lo_lanes = lax.broadcasted_iota(jnp.int32, (1, LANES), 1) < SB_DH

    @pl.when(grp == 0)
    def _():
        for sq in seqs:
            kn_scr[sq] = _pair_rms(k_ref[sq].astype(F32), kg_ref[...], lo_lanes).astype(BF16)

        def transpose_block(kb, carry):
            rows = pl.ds(pl.multiple_of(kb * bk, bk), bk)
            for sq in seqs:
                vt_scr[sq, kb] = v_ref[sq, rows, :].astype(F32).T.astype(BF16)
            return carry

        lax.fori_loop(0, n_kb, transpose_block, 0)

        key = lax.broadcasted_iota(jnp.int32, (bk, bk), 0)
        qry = lax.broadcasted_iota(jnp.int32, (bk, bk), 1)
        bias_scr[...] = jnp.where(key < qry, 0.0, MASK_BIAS)

    for sq in seqs:
        qn = _pair_rms(q_ref[sq].astype(F32), qg_ref[...], lo_lanes) * (SB_DH ** -0.5 * LOG2_E)
        q2t_scr[sq, :, :gq] = jnp.where(lo_lanes, qn, 0.0).T.astype(BF16)
        q2t_scr[sq, :, gq:] = jnp.where(lo_lanes, 0.0, qn).T.astype(BF16)
    acc_scr[...] = jnp.zeros_like(acc_scr)
    c_scr[...] = jnp.zeros_like(c_scr)

    last = grp * SB_QGROUP + SB_QGROUP - 1

    def score_columns(sq, kt, u, cols, diagonal):
        z = _dot(kt, q2t_scr[sq, :, cols])
        if diagonal:
            parts = [z[:, :bk] + bias_scr[...]] + ([z[:, bk:]] if z.shape[1] > bk else [])
            z = jnp.concatenate(parts, axis=1)
        sp = jnp.maximum(z, 0.0) + jnp.log2(1.0 + jnp.exp2(-jnp.abs(z)))
        sp_scr[sq, u, :, cols] = sp.astype(BF16)
        zs_scr[sq, u, :, cols] = z - sp
        sp0_scr[sq, u, :, cols] = sp[0:8]

    def stage_scores(p, masked):
        for u in range(SB_UNROLL):
            m = p * SB_UNROLL + u
            for sq in seqs:
                kt = kn_scr[sq, pl.ds(pl.multiple_of((last - m) * bk, bk), bk), :]
                if not masked:
                    score_columns(sq, kt, u, slice(0, 2 * gq), False)
                    continue
                lo = (SB_QGROUP - 1 - m) * bk
                for half in (0, gq):
                    score_columns(sq, kt, u, slice(half + lo, half + gq), True)
                    if lo:
                        hidden = slice(half, half + lo)
                        sp_scr[sq, u, :, hidden] = jnp.zeros((bk, lo), BF16)
                        zs_scr[sq, u, :, hidden] = jnp.full((bk, lo), MASK_BIAS, F32)
                        sp0_scr[sq, u, :, hidden] = jnp.zeros((8, lo), F32)

    def stage_weights(p):
        c = [c_scr[sq] for sq in seqs]
        for u in range(SB_UNROLL):
            for sq in seqs:
                later = _dot(ut_ref[...], sp_scr[sq, u])
                w_scr[sq, u] = jnp.exp2(zs_scr[sq, u] - later - c[sq]).astype(BF16)
                c[sq] = c[sq] + later[0:1] + sp0_scr[sq, u, 0:1, :]
        for sq in seqs:
            c_scr[sq] = c[sq]

    def stage_values(p):
        for u in range(SB_UNROLL):
            for sq in seqs:
                vt = vt_scr[sq, last - (p * SB_UNROLL + u)]
                w = w_scr[sq, u]
                acc_scr[sq, 0:SB_DH, :] += _dot(vt[0:SB_DH], w[:, :gq])
                acc_scr[sq, SB_DH:, :] += _dot(vt[SB_DH:], w[:, gq:])

    n_masked = SB_QGROUP // SB_UNROLL
    for p in range(n_masked):
        if p >= 2:
            stage_values(p - 2)
        if p >= 1:
            stage_weights(p - 1)
        stage_scores(p, True)

    n_trips = grp * (n_masked // SB_LOOP_STEPS)

    def keep_going(carry):
        trip, c_min = carry
        return jnp.logical_and(trip < n_trips, c_min < SB_DEAD_LOG2)

    def body(carry):
        trip, _ = carry
        for s in range(SB_LOOP_STEPS):
            p = n_masked + trip * SB_LOOP_STEPS + s
            stage_values(p - 2)
            stage_weights(p - 1)
            stage_scores(p, False)
        return trip + 1, jnp.min(c_scr[...])

    trips, c_min = lax.while_loop(keep_going, body, (jnp.int32(0), jnp.float32(0.0)))
    p_end = n_masked + trips * SB_LOOP_STEPS
    stage_values(p_end - 2)

    @pl.when(c_min < SB_DEAD_LOG2)
    def _():
        stage_weights(p_end - 1)
        stage_values(p_end - 1)

    for sq in seqs:
        o_ref[sq] = _pair_rms(acc_scr[sq].T, og_ref[...], lo_lanes).astype(o_ref.dtype)


def _sb_attention(proj, qg, kg, og, *, batch):
    m, cols = proj.shape
    t = m // batch
    bk = SB_BLOCK
    gq = SB_QGROUP * bk
    n_groups = t // gq
    n_pairs = SB_HEADS * SB_DH // LANES
    ns = SB_SEQS
    oq, ok, ov = (_SEG[s][1] // LANES for s in ("sq", "sk", "sv"))
    j = np.arange(bk)
    ut = jnp.asarray((j[None, :] > j[:, None]).astype(np.float32), BF16)
    pair = lambda g: jnp.tile(g, 2).reshape(1, LANES)
    const = lambda shape: pl.BlockSpec(shape, lambda b, p, i: (0, 0))
    proj3 = proj.reshape(batch, t, cols)
    out = pl.pallas_call(
        _sb_kernel,
        grid=(batch // ns, n_pairs, n_groups),
        in_specs=[pl.BlockSpec((ns, gq, LANES), lambda b, p, i: (b, i, oq + p)),
                  pl.BlockSpec((ns, t, LANES), lambda b, p, i: (b, 0, ok + p)),
                  pl.BlockSpec((ns, t, LANES), lambda b, p, i: (b, 0, ov + p)),
                  const((1, LANES)), const((1, LANES)), const((1, LANES)),
                  const((bk, bk))],
        out_specs=pl.BlockSpec((ns, gq, LANES), lambda b, p, i: (b, i, p)),
        out_shape=jax.ShapeDtypeStruct((batch, t, n_pairs * LANES), BF16),
        scratch_shapes=[pltpu.VMEM((ns, t, LANES), BF16),
                        pltpu.VMEM((ns, t // bk, LANES, bk), BF16),
                        pltpu.VMEM((ns, LANES, 2 * gq), BF16),
                        pltpu.VMEM((ns, 1, 2 * gq), F32),
                        pltpu.VMEM((ns, LANES, gq), F32),
                        pltpu.VMEM((bk, bk), F32),
                        pltpu.VMEM((ns, SB_UNROLL, bk, 2 * gq), BF16),
                        pltpu.VMEM((ns, SB_UNROLL, bk, 2 * gq), F32),
                        pltpu.VMEM((ns, SB_UNROLL, 8, 2 * gq), F32),
                        pltpu.VMEM((ns, SB_UNROLL, bk, 2 * gq), BF16)],
        compiler_params=pltpu.CompilerParams(
            dimension_semantics=("arbitrary", "arbitrary", "arbitrary"),
            vmem_limit_bytes=VMEM_LIMIT),
        name="sb_attention",
    )(proj3, proj3, proj3, pair(qg), pair(kg), pair(og), ut)
    return out.reshape(m, n_pairs * LANES)


def kernel(x, norm_mix_g, w_in, gla_w_decay, gla_b_decay, gla_out_g, sb_q_g, sb_k_g, sb_out_g,
           hg_out_g, hg_lb_logits, w_out, norm_ffn_g, w_ffn_up, w_ffn_down):
    batch, seq, d_model = x.shape
    depth = w_in.shape[0]
    x2 = x.reshape(batch * seq, d_model).astype(F32)

    gla_cols = _padded_head_cols(0, GLA_HEADS, GLA_DK, GLA_DK_PAD)
    sb_lo = GLA_HEADS * GLA_DV
    hg_lo = sb_lo + SB_HEADS * SB_DH
    w_main = _take_padded(w_in, _MAIN_SRC, 2).astype(BF16)
    w_gate = _take_padded(w_in, _GATE_SRC, 2).astype(BF16)
    wd = _take_padded(gla_w_decay.astype(F32), gla_cols, 2)
    wd = jnp.pad(wd, ((0, 0), (0, LANES - GLA_LOWRANK), (0, 0)))
    bd = _take_padded(gla_b_decay.astype(F32), gla_cols, 1)[:, None, :]
    gain_a = jnp.pad(gla_out_g.astype(F32), ((0, 0), (0, HEAD_V_PAD - GLA_DV)))[:, None, :]
    gain_c = jnp.tile(hg_out_g.astype(F32), (1, LANES // HG_DV))[:, None, :]
    w_a = _take_padded(w_out, _padded_head_cols(0, GLA_HEADS, GLA_DV, HEAD_V_PAD), 1).astype(BF16)
    w_b = w_out[:, sb_lo:hg_lo].astype(BF16)
    w_c = w_out[:, hg_lo:].astype(BF16)
    w_up = w_ffn_up.astype(BF16)
    w_down = w_ffn_down.astype(BF16)
    lb_logits = hg_lb_logits.astype(F32)
    g_mix = norm_mix_g.astype(F32)[:, None, :]
    g_ffn = norm_ffn_g.astype(F32)[:, None, :]

    for li in range(depth):
        projs = _in_proj(li, x2, g_mix, w_main, w_gate, tm=ROW_TILE, tn=MAIN_COLS // 3)
        o_a, o_c = _mixers(li, projs, wd, bd, gain_a, lb_logits, gain_c, batch=batch)
        o_b = _sb_attention(projs[0], sb_q_g[li].astype(F32), sb_k_g[li].astype(F32),
                            sb_out_g[li].astype(F32), batch=batch)
        x2 = _residual_matmul(li, x2, (o_a, o_b, o_c), (w_a, w_b, w_c), tm=ROW_TILE)
        act = _ffn_up(li, x2, g_ffn, w_up, tm=ROW_TILE, tn=w_up.shape[2] // 4)
        x2 = _residual_matmul(li, x2, (act,), (w_down,), tm=ROW_TILE // 2)
    return x2.reshape(batch, seq, d_model).astype(x.dtype)
```

```python
import functools

import numpy as np
import jax
import jax.numpy as jnp
from jax import lax
from jax.experimental import pallas as pl
from jax.experimental.pallas import tpu as pltpu

F32 = jnp.float32
BF16 = jnp.bfloat16

LANES = 128
RMS_EPS = 1e-6

GLA_HEADS, GLA_DK, GLA_DV, GLA_LOWRANK = 4, 48, 96, 16
GLA_GATE_NORMALIZER = 16.0
SB_HEADS, SB_DH, SB_BLOCK = 6, 64, 128
HG_HEADS, HG_DK, HG_DV = 4, 128, 64

GLA_DK_PAD = 64
HEAD_V_PAD = LANES
MIX_CHUNK = 128
SB_QGROUP = 4
SB_UNROLL = 2
SB_LOOP_STEPS = 1
SB_SEQS = 4
MIX_SEQS = 2
SB_DEAD_LOG2 = 150.0
LOG2_E = 1.4426950408889634
MASK_BIAS = -1e30
assert SB_QGROUP % SB_UNROLL == 0 and SB_QGROUP // SB_UNROLL >= 2
assert (SB_QGROUP // SB_UNROLL) % SB_LOOP_STEPS == 0
ROW_TILE = 1024
VMEM_LIMIT = 56 * 1024 * 1024

_MAIN_SEGS = (("gv", 512), ("gg", 512), ("hq", 512), ("hi", 256), ("hg", 256),
              ("gq", 256), ("gk", 256), ("sq", 384), ("sk", 384), ("sv", 384), ("pad", 128))
_GATE_SEGS = (("hf", 512), ("glr", 128))
_SEG = {}
for _arr, _segs in enumerate((_MAIN_SEGS, _GATE_SEGS)):
    _off = 0
    for _name, _w in _segs:
        _SEG[_name] = (_arr, _off, _w)
        _off += _w
MAIN_COLS = sum(w for _, w in _MAIN_SEGS)
GATE_COLS = sum(w for _, w in _GATE_SEGS)


def _padded_head_cols(start, heads, width, pad):
    idx = -np.ones((heads, pad), np.int64)
    idx[:, :width] = start + np.arange(heads)[:, None] * width + np.arange(width)[None, :]
    return idx.reshape(-1)


def _proj_source_columns():
    sizes = (GLA_HEADS * GLA_DK, GLA_HEADS * GLA_DK, GLA_HEADS * GLA_DV, GLA_LOWRANK,
             GLA_HEADS * GLA_DV, SB_HEADS * SB_DH, SB_HEADS * SB_DH, SB_HEADS * SB_DH,
             HG_HEADS * HG_DK, HG_HEADS * HG_DK, HG_HEADS * HG_DV, HG_HEADS * HG_DV)
    starts = np.concatenate([[0], np.cumsum(sizes)[:-1]])
    (gq, gk, gv, glr, gg, sq, sk, sv, hq, hf, hi, hg) = [int(s) for s in starts]
    src = {
        "gq": _padded_head_cols(gq, GLA_HEADS, GLA_DK, GLA_DK_PAD),
        "gk": _padded_head_cols(gk, GLA_HEADS, GLA_DK, GLA_DK_PAD),
        "gv": _padded_head_cols(gv, GLA_HEADS, GLA_DV, HEAD_V_PAD),
        "gg": _padded_head_cols(gg, GLA_HEADS, GLA_DV, HEAD_V_PAD),
        "glr": _padded_head_cols(glr, 1, GLA_LOWRANK, LANES),
        "sq": np.arange(sq, sq + SB_HEADS * SB_DH),
        "sk": np.arange(sk, sk + SB_HEADS * SB_DH),
        "sv": np.arange(sv, sv + SB_HEADS * SB_DH),
        "hq": np.arange(hq, hq + HG_HEADS * HG_DK),
        "hf": np.arange(hf, hf + HG_HEADS * HG_DK),
        "hi": np.arange(hi, hi + HG_HEADS * HG_DV),
        "hg": np.arange(hg, hg + HG_HEADS * HG_DV),
        "pad": -np.ones(LANES, np.int64),
    }
    return tuple(np.concatenate([src[name] for name, _ in segs]) for segs in (_MAIN_SEGS, _GATE_SEGS))


_MAIN_SRC, _GATE_SRC = _proj_source_columns()


def _take_padded(arr, src, axis):
    pad = src < 0
    breaks = np.flatnonzero(np.where(pad[1:] | pad[:-1], pad[1:] != pad[:-1], np.diff(src) != 1)) + 1
    pieces = []
    for run in np.split(src, breaks):
        if run[0] < 0:
            shape = list(arr.shape)
            shape[axis] = len(run)
            pieces.append(jnp.zeros(shape, arr.dtype))
        else:
            pieces.append(lax.slice_in_dim(arr, int(run[0]), int(run[-1]) + 1, axis=axis))
    return jnp.concatenate(pieces, axis=axis)


def _dot(a, b):
    return jnp.dot(a, b, preferred_element_type=F32)


def _split_bf16(x):
    hi = x.astype(BF16)
    lo = (x - hi.astype(F32)).astype(BF16)
    return hi, lo


def _softplus(z):
    return jnp.maximum(z, 0.0) + jnp.log(1.0 + jnp.exp(-jnp.abs(z)))


def _silu(z):
    return z / (1.0 + jnp.exp(-z))


def _rms_rows(x, g):
    ms = jnp.mean(x * x, axis=-1, keepdims=True)
    return x * lax.rsqrt(ms + RMS_EPS) * g


def _with_next_norm(n_col, x_ref, xn_ref, g_ref, h_scr, body):
    i, j = pl.program_id(0), pl.program_id(1)
    rows = x_ref.shape[0] // n_col

    @pl.when(jnp.logical_and(i == 0, j == 0))
    def _():
        h_scr[0] = _rms_rows(x_ref[...], g_ref[...]).astype(BF16)

    for slot in (0, 1):
        @pl.when(lax.rem(i, 2) == slot)
        def _():
            body(h_scr[slot])
            r = pl.ds(pl.multiple_of(j * rows, rows), rows)
            h_scr[1 - slot, r, :] = _rms_rows(xn_ref[r, :], g_ref[...]).astype(BF16)


def _in_proj_kernel(n_col, x_ref, xn_ref, g_ref, w_ref, wg_ref, o_ref, og_ref, h_scr):
    def body(h):
        o_ref[...] = _dot(h, w_ref[...]).astype(o_ref.dtype)

        @pl.when(pl.program_id(1) == 0)
        def _():
            og_ref[...] = _dot(h, wg_ref[...])

    _with_next_norm(n_col, x_ref, xn_ref, g_ref, h_scr, body)


def _next_row_tile(n_rows):
    return lambda i, j: (jnp.minimum(i + 1, n_rows - 1), 0)


def _in_proj(layer, x, g, w_main, w_gate, *, tm, tn):
    m, d = x.shape
    n, ng = w_main.shape[2], w_gate.shape[2]
    assert tm % (n // tn) == 0
    return pl.pallas_call(
        functools.partial(_in_proj_kernel, n // tn),
        grid=(m // tm, n // tn),
        in_specs=[pl.BlockSpec((tm, d), lambda i, j: (i, 0)),
                  pl.BlockSpec((tm, d), _next_row_tile(m // tm)),
                  pl.BlockSpec((None, 1, d), lambda i, j: (layer, 0, 0)),
                  pl.BlockSpec((None, d, tn), lambda i, j: (layer, 0, j)),
                  pl.BlockSpec((None, d, ng), lambda i, j: (layer, 0, 0))],
        out_specs=[pl.BlockSpec((tm, tn), lambda i, j: (i, j)),
                   pl.BlockSpec((tm, ng), lambda i, j: (i, 0))],
        out_shape=[jax.ShapeDtypeStruct((m, n), BF16), jax.ShapeDtypeStruct((m, ng), F32)],
        scratch_shapes=[pltpu.VMEM((2, tm, d), BF16)],
        compiler_params=pltpu.CompilerParams(
            dimension_semantics=("arbitrary", "arbitrary"), vmem_limit_bytes=VMEM_LIMIT),
        name="in_proj",
    )(x, x, g, w_main, w_gate)


def _ffn_up_kernel(n_col, x_ref, xn_ref, g_ref, wg_ref, wu_ref, o_ref, h_scr):
    def body(h):
        gate = _dot(h, wg_ref[...])
        up = _dot(h, wu_ref[...])
        o_ref[...] = (_silu(gate) * up).astype(o_ref.dtype)

    _with_next_norm(n_col, x_ref, xn_ref, g_ref, h_scr, body)


def _ffn_up(layer, x, g, w_up, *, tm, tn):
    m, d = x.shape
    d_ff = w_up.shape[2] // 2
    nj = d_ff // tn
    assert tm % nj == 0
    return pl.pallas_call(
        functools.partial(_ffn_up_kernel, nj),
        grid=(m // tm, nj),
        in_specs=[pl.BlockSpec((tm, d), lambda i, j: (i, 0)),
                  pl.BlockSpec((tm, d), _next_row_tile(m // tm)),
                  pl.BlockSpec((None, 1, d), lambda i, j: (layer, 0, 0)),
                  pl.BlockSpec((None, d, tn), lambda i, j: (layer, 0, j)),
                  pl.BlockSpec((None, d, tn), lambda i, j: (layer, 0, j + nj))],
        out_specs=pl.BlockSpec((tm, tn), lambda i, j: (i, j)),
        out_shape=jax.ShapeDtypeStruct((m, d_ff), BF16),
        scratch_shapes=[pltpu.VMEM((2, tm, d), BF16)],
        compiler_params=pltpu.CompilerParams(
            dimension_semantics=("arbitrary", "arbitrary"), vmem_limit_bytes=VMEM_LIMIT),
        name="ffn_up",
    )(x, x, g, w_up, w_up)


def _residual_matmul_kernel(n_in, res_ref, *refs):
    a_refs, w_refs, o_ref = refs[:n_in], refs[n_in:2 * n_in], refs[2 * n_in]
    acc = res_ref[...]
    for a_ref, w_ref in zip(a_refs, w_refs):
        acc = acc + _dot(a_ref[...], w_ref[...])
    o_ref[...] = acc


def _residual_matmul(layer, res, acts, weights, *, tm):
    m, n = res.shape
    n_in = len(acts)
    in_specs = [pl.BlockSpec((tm, n), lambda i: (i, 0))]
    in_specs += [pl.BlockSpec((tm, a.shape[1]), lambda i: (i, 0)) for a in acts]
    in_specs += [pl.BlockSpec((None,) + w.shape[1:], lambda i: (layer, 0, 0)) for w in weights]
    return pl.pallas_call(
        functools.partial(_residual_matmul_kernel, n_in),
        grid=(m // tm,),
        in_specs=in_specs,
        out_specs=pl.BlockSpec((tm, n), lambda i: (i, 0)),
        out_shape=jax.ShapeDtypeStruct((m, n), F32),
        compiler_params=pltpu.CompilerParams(
            dimension_semantics=("arbitrary",), vmem_limit_bytes=VMEM_LIMIT),
        name="residual_matmul",
    )(res, *acts, *weights)


def _pair_rms(x, g, lo_lanes):
    x2 = x * x
    s_lo = jnp.sum(jnp.where(lo_lanes, x2, 0.0), axis=-1, keepdims=True)
    s_hi = jnp.sum(jnp.where(lo_lanes, 0.0, x2), axis=-1, keepdims=True)
    ms = jnp.where(lo_lanes, s_lo, s_hi) * (2.0 / LANES)
    return x * lax.rsqrt(ms + RMS_EPS) * g


def _decay_sum_matrix(chunk):
    t = np.arange(chunk)
    blocks = [(t[None, :] <= t[:, None])]
    h = 1
    while h < chunk:
        mid = (t // (2 * h)) * (2 * h) + h
        right = t >= mid
        m = np.where(right[:, None],
                     (t[None, :] >= mid[:, None]) & (t[None, :] <= t[:, None]),
                     (t[None, :] > t[:, None]) & (t[None, :] < mid[:, None]))
        blocks.append(m)
        h *= 2
    return np.concatenate(blocks, axis=0).astype(np.float32)


def _pair_owner_matrix(chunk):
    t = np.arange(chunk)
    x = t[:, None] ^ t[None, :]
    level = np.floor(np.log2(np.maximum(x, 1))).astype(np.int32)
    return np.where(x == 0, 0, np.where(t[None, :] < t[:, None], 1 + level, -1)).astype(np.int32)


def _chunk_step(qb, kb, lf, v_ref, gate_ref, gain, first_chunk, msum_ref, owner_ref,
                rd, wr, state_ref, o_ref, *, heads, dk_lanes, dv, dv_lanes):
    q_rd, k_rd, dec_rd = rd
    q_wr, k_wr, dec_wr = wr
    c = qb.shape[0]
    n_levels = c.bit_length() - 1
    n_tiles = n_levels + 1
    lf_b = lf.astype(BF16)

    def decay_rows(block):
        return jnp.exp(_dot(msum_ref[block * c:(block + 1) * c, :], lf_b).astype(BF16))

    def store_slot(t):
        if t == 0:
            q_wr[0] = qb
            k_wr[0] = kb.T
        elif t <= n_levels:
            e = decay_rows(t)
            q_wr[t] = qb * e
            k_wr[t] = (kb * e).T
        else:
            prefix = _dot(msum_ref[0:c, :], lf_b)
            total = prefix[c - 1:c, :]
            q_wr[t] = qb * jnp.exp(prefix.astype(BF16))
            k_wr[t] = (kb * jnp.exp((total - prefix).astype(BF16))).T
            dec_wr[...] = jnp.exp(jnp.broadcast_to(total, (LANES, total.shape[1])).T)

    owner = owner_ref[...]
    owned = [owner == t for t in range(n_tiles)]
    lane = lax.broadcasted_iota(jnp.int32, (1, LANES), 1)

    for hd in range(heads):
        g0 = (hd * dk_lanes // LANES) * LANES
        grp = slice(g0, g0 + LANES)
        if dk_lanes < LANES:
            lo = hd * dk_lanes - g0
            head_lanes = jnp.where((lane >= lo) & (lane < lo + dk_lanes), 1.0, 0.0).astype(BF16)
            pick = lambda t: q_rd[t, :, grp] * head_lanes
        else:
            pick = lambda t: q_rd[t, :, grp]
        scores = jnp.zeros((c, c), F32)
        for t in range(n_tiles):
            scores = jnp.where(owned[t], _dot(pick(t), k_rd[t, grp, :]), scores)
        v0 = (hd * dv_lanes // LANES) * LANES
        sl = slice(v0, v0 + LANES)
        v_h = v_ref[:, sl]
        state = jnp.where(first_chunk, 0.0, state_ref[hd])
        o = _dot(scores.astype(BF16), v_h) + _dot(pick(n_tiles), state.astype(BF16))
        state_ref[hd] = state * dec_rd[grp, :] + _dot(k_rd[n_tiles, grp, :], v_h)
        if dv_lanes == LANES:
            ms = jnp.sum(o * o, axis=-1, keepdims=True) * (1.0 / dv)
            y = o * lax.rsqrt(ms + RMS_EPS) * gain
        elif hd % 2 == 0:
            o_even = o
        else:
            y = _pair_rms(jnp.where(lane < dv_lanes, o_even, o), gain, lane < dv_lanes)
        if dv_lanes == LANES or hd % 2 == 1:
            o_ref[:, sl] = (y * _silu(gate_ref[:, sl].astype(F32))).astype(o_ref.dtype)

        for t in range(hd, n_tiles + 1, heads):
            store_slot(t)
        yield


def _mixer_step(scratch, step_fn):
    @pl.when(pl.program_id(1) == 0)
    def _():
        for r in scratch:
            r[...] = jnp.zeros_like(r)

    parity = lax.rem(pl.program_id(1), 2)
    for buf in (0, 1):
        @pl.when(parity == buf)
        def _():
            step_fn(1 - buf, buf)


def _mixers_kernel(layer,
                   gq_ref, gk_ref, lr_ref, wd_ref, bd_ref, hq_ref, hf_ref, lb_ref,
                   gv_ref, gg_ref, gain_a_ref, hi_ref, hg_ref, gain_c_ref, msum_ref, owner_ref,
                   oa_ref, oc_ref,
                   state_a, q_a, k_a, dec_a, state_c, q_c, k_c, dec_c):
    first_chunk = pl.program_id(1) == 1
    seqs = range(gq_ref.shape[0])

    def step(rd, wr):
        wd_hi, wd_lo = _split_bf16(wd_ref[...])
        lb_logits = lb_ref[...]
        ex = jnp.exp(lb_logits - jnp.max(lb_logits, axis=0, keepdims=True))
        probs = ex / jnp.sum(ex, axis=0, keepdims=True)
        lb = jnp.zeros_like(probs[0:1])
        for d in range(1, layer + 1):
            lb = lb + probs[d:d + 1]

        chains = []
        for sq in seqs:
            lr_hi, lr_lo = _split_bf16(lr_ref[sq])
            logits = _dot(lr_hi, wd_hi) + _dot(lr_hi, wd_lo) + _dot(lr_lo, wd_hi) + bd_ref[...]
            lf_a = -_softplus(-logits) * (1.0 / GLA_GATE_NORMALIZER)
            qb_a = (gq_ref[sq].astype(F32) * GLA_DK ** -0.5).astype(BF16)
            chains.append(_chunk_step(
                qb_a, gk_ref[sq], lf_a, gv_ref.at[sq], gg_ref.at[sq], gain_a_ref[...], first_chunk,
                msum_ref, owner_ref, (q_a.at[rd].at[sq], k_a.at[rd].at[sq], dec_a.at[rd].at[sq]),
                (q_a.at[wr].at[sq], k_a.at[wr].at[sq], dec_a.at[wr].at[sq]), state_a.at[sq],
                oa_ref.at[sq], heads=GLA_HEADS, dk_lanes=GLA_DK_PAD, dv=GLA_DV, dv_lanes=HEAD_V_PAD))

            hf = hf_ref[sq]
            sp = _softplus(-hf)
            log_sig = -sp
            a = jnp.log(jnp.maximum(lb, 1e-30))
            b = jnp.log(1.0 - lb) + log_sig
            lae = jnp.maximum(a, b) + jnp.log(1.0 + jnp.exp(-jnp.abs(a - b)))
            lf_c = jnp.where(lb > 0.0, lae, log_sig)
            kb_c = ((1.0 - lb) * jnp.exp(-(hf + sp))).astype(BF16)
            chains.append(_chunk_step(
                hq_ref[sq], kb_c, lf_c, hi_ref.at[sq], hg_ref.at[sq], gain_c_ref[...], first_chunk,
                msum_ref, owner_ref, (q_c.at[rd].at[sq], k_c.at[rd].at[sq], dec_c.at[rd].at[sq]),
                (q_c.at[wr].at[sq], k_c.at[wr].at[sq], dec_c.at[wr].at[sq]), state_c.at[sq],
                oc_ref.at[sq], heads=HG_HEADS, dk_lanes=HG_DK, dv=HG_DV, dv_lanes=HG_DV))
        for _ in zip(*chains):
            pass

    _mixer_step((state_a, q_a, k_a, dec_a, state_c, q_c, k_c, dec_c), step)


def _mixers(layer, projs, wd, bd, gain_a, lb_logits, gain_c, *, batch):
    assert GLA_HEADS == HG_HEADS
    m = projs[0].shape[0]
    t = m // batch
    n_chunks = t // MIX_CHUNK
    ns = MIX_SEQS
    msum = jnp.asarray(_decay_sum_matrix(MIX_CHUNK), BF16)
    owner = jnp.asarray(_pair_owner_matrix(MIX_CHUNK))
    lead = lambda s: jnp.minimum(s, n_chunks - 1)
    lag = lambda s: jnp.maximum(s - 1, 0)
    projs3 = [p.reshape(batch, t, p.shape[1]) for p in projs]

    def seg(name, row_of):
        arr, off, width = _SEG[name]
        assert off % width == 0
        spec = pl.BlockSpec((ns, MIX_CHUNK, width), lambda g, s: (g, row_of(s), off // width))
        return projs3[arr], spec

    const = lambda e: (e, pl.BlockSpec(e.shape, lambda g, s: (0, 0)))
    per_layer = lambda e: (e, pl.BlockSpec((None,) + e.shape[1:], lambda g, s: (layer, 0, 0)))
    operands = [seg("gq", lead), seg("gk", lead), seg("glr", lead), per_layer(wd), per_layer(bd),
                seg("hq", lead), seg("hf", lead), const(lb_logits),
                seg("gv", lag), seg("gg", lag), per_layer(gain_a),
                seg("hi", lag), seg("hg", lag), per_layer(gain_c), const(msum), const(owner)]
    widths = (GLA_HEADS * HEAD_V_PAD, HG_HEADS * HG_DV)
    n_slots = MIX_CHUNK.bit_length() + 1

    def scratch(dk_lanes):
        w = GLA_HEADS * dk_lanes
        return [pltpu.VMEM((ns, GLA_HEADS, LANES, HEAD_V_PAD), F32),
                pltpu.VMEM((2, ns, n_slots, MIX_CHUNK, w), BF16),
                pltpu.VMEM((2, ns, n_slots, w, MIX_CHUNK), BF16),
                pltpu.VMEM((2, ns, w, LANES), F32)]

    out_spec = lambda width: pl.BlockSpec((ns, MIX_CHUNK, width), lambda g, s: (g, lag(s), 0))
    outs = pl.pallas_call(
        functools.partial(_mixers_kernel, layer),
        grid=(batch // ns, n_chunks + 1),
        in_specs=[spec for _, spec in operands],
        out_specs=[out_spec(w) for w in widths],
        out_shape=[jax.ShapeDtypeStruct((batch, t, w), BF16) for w in widths],
        scratch_shapes=scratch(GLA_DK_PAD) + scratch(HG_DK),
        compiler_params=pltpu.CompilerParams(
            dimension_semantics=("arbitrary", "arbitrary"), vmem_limit_bytes=VMEM_LIMIT),
        name="mixers",
    )(*[arr for arr, _ in operands])
    return [o.reshape(m, o.shape[2]) for o in outs]


def _sb_kernel(q_ref, k_ref, v_ref, qg_ref, kg_ref, og_ref, ut_ref, o_ref,
               kn_scr, vt_scr, q2t_scr, c_scr, acc_scr,
               bias_scr, sp_scr, zs_scr, sp0_scr, w_scr):
    grp = pl.program_id(2)
    bk = SB_BLOCK
    seqs = range(q_ref.shape[0])
    gq = q_ref.shape[1]
    n_kb = k_ref.shape[1] // bk
    lo_lanes = lax.broadcasted_iota(jnp.int32, (1, LANES), 1) < SB_DH

    @pl.when(grp == 0)
    def _():
        for sq in seqs:
            kn_scr[sq] = _pair_rms(k_ref[sq].astype(F32), kg_ref[...], lo_lanes).astype(BF16)

        def transpose_block(kb, carry):
            rows = pl.ds(pl.multiple_of(kb * bk, bk), bk)
            for sq in seqs:
                vt_scr[sq, kb] = v_ref[sq, rows, :].astype(F32).T.astype(BF16)
            return carry

        lax.fori_loop(0, n_kb, transpose_block, 0)

        key = lax.broadcasted_iota(jnp.int32, (bk, bk), 0)
        qry = lax.broadcasted_iota(jnp.int32, (bk, bk), 1)
        bias_scr[...] = jnp.where(key < qry, 0.0, MASK_BIAS)

    for sq in seqs:
        qn = _pair_rms(q_ref[sq].astype(F32), qg_ref[...], lo_lanes) * (SB_DH ** -0.5 * LOG2_E)
        q2t_scr[sq, :, :gq] = jnp.where(lo_lanes, qn, 0.0).T.astype(BF16)
        q2t_scr[sq, :, gq:] = jnp.where(lo_lanes, 0.0, qn).T.astype(BF16)
    acc_scr[...] = jnp.zeros_like(acc_scr)
    c_scr[...] = jnp.zeros_like(c_scr)

    last = grp * SB_QGROUP + SB_QGROUP - 1

    def score_columns(sq, kt, u, cols, diagonal):
        z = _dot(kt, q2t_scr[sq, :, cols])
        if diagonal:
            parts = [z[:, :bk] + bias_scr[...]] + ([z[:, bk:]] if z.shape[1] > bk else [])
            z = jnp.concatenate(parts, axis=1)
        sp = jnp.maximum(z, 0.0) + jnp.log2(1.0 + jnp.exp2(-jnp.abs(z)))
        sp_scr[sq, u, :, cols] = sp.astype(BF16)
        zs_scr[sq, u, :, cols] = z - sp
        sp0_scr[sq, u, :, cols] = sp[0:8]

    def stage_scores(p, masked):
        for u in range(SB_UNROLL):
            m = p * SB_UNROLL + u
            for sq in seqs:
                kt = kn_scr[sq, pl.ds(pl.multiple_of((last - m) * bk, bk), bk), :]
                if not masked:
                    score_columns(sq, kt, u, slice(0, 2 * gq), False)
                    continue
                lo = (SB_QGROUP - 1 - m) * bk
                for half in (0, gq):
                    score_columns(sq, kt, u, slice(half + lo, half + gq), True)
                    if lo:
                        hidden = slice(half, half + lo)
                        sp_scr[sq, u, :, hidden] = jnp.zeros((bk, lo), BF16)
                        zs_scr[sq, u, :, hidden] = jnp.full((bk, lo), MASK_BIAS, F32)
                        sp0_scr[sq, u, :, hidden] = jnp.zeros((8, lo), F32)

    def stage_weights(p):
        c = [c_scr[sq] for sq in seqs]
        for u in range(SB_UNROLL):
            for sq in seqs:
                later = _dot(ut_ref[...], sp_scr[sq, u])
                w_scr[sq, u] = jnp.exp2(zs_scr[sq, u] - later - c[sq]).astype(BF16)
                c[sq] = c[sq] + later[0:1] + sp0_scr[sq, u, 0:1, :]
        for sq in seqs:
            c_scr[sq] = c[sq]

    def stage_values(p):
        for u in range(SB_UNROLL):
            for sq in seqs:
                vt = vt_scr[sq, last - (p * SB_UNROLL + u)]
                w = w_scr[sq, u]
                acc_scr[sq, 0:SB_DH, :] += _dot(vt[0:SB_DH], w[:, :gq])
                acc_scr[sq, SB_DH:, :] += _dot(vt[SB_DH:], w[:, gq:])

    n_masked = SB_QGROUP // SB_UNROLL
    for p in range(n_masked):
        if p >= 2:
            stage_values(p - 2)
        if p >= 1:
            stage_weights(p - 1)
        stage_scores(p, True)

    n_trips = grp * (n_masked // SB_LOOP_STEPS)

    def keep_going(carry):
        trip, c_min = carry
        return jnp.logical_and(trip < n_trips, c_min < SB_DEAD_LOG2)

    def body(carry):
        trip, _ = carry
        for s in range(SB_LOOP_STEPS):
            p = n_masked + trip * SB_LOOP_STEPS + s
            stage_values(p - 2)
            stage_weights(p - 1)
            stage_scores(p, False)
        return trip + 1, jnp.min(c_scr[...])

    trips, c_min = lax.while_loop(keep_going, body, (jnp.int32(0), jnp.float32(0.0)))
    p_end = n_masked + trips * SB_LOOP_STEPS
    stage_values(p_end - 2)

    @pl.when(c_min < SB_DEAD_LOG2)
    def _():
        stage_weights(p_end - 1)
        stage_values(p_end - 1)

    for sq in seqs:
        o_ref[sq] = _pair_rms(acc_scr[sq].T, og_ref[...], lo_lanes).astype(o_ref.dtype)


def _sb_attention(proj, qg, kg, og, *, batch):
    m, cols = proj.shape
    t = m // batch
    bk = SB_BLOCK
    gq = SB_QGROUP * bk
    n_groups = t // gq
    n_pairs = SB_HEADS * SB_DH // LANES
    ns = SB_SEQS
    oq, ok, ov = (_SEG[s][1] // LANES for s in ("sq", "sk", "sv"))
    j = np.arange(bk)
    ut = jnp.asarray((j[None, :] > j[:, None]).astype(np.float32), BF16)
    pair = lambda g: jnp.tile(g, 2).reshape(1, LANES)
    const = lambda shape: pl.BlockSpec(shape, lambda b, p, i: (0, 0))
    proj3 = proj.reshape(batch, t, cols)
    out = pl.pallas_call(
        _sb_kernel,
        grid=(batch // ns, n_pairs, n_groups),
        in_specs=[pl.BlockSpec((ns, gq, LANES), lambda b, p, i: (b, i, oq + p)),
                  pl.BlockSpec((ns, t, LANES), lambda b, p, i: (b, 0, ok + p)),
                  pl.BlockSpec((ns, t, LANES), lambda b, p, i: (b, 0, ov + p)),
                  const((1, LANES)), const((1, LANES)), const((1, LANES)),
                  const((bk, bk))],
        out_specs=pl.BlockSpec((ns, gq, LANES), lambda b, p, i: (b, i, p)),
        out_shape=jax.ShapeDtypeStruct((batch, t, n_pairs * LANES), BF16),
        scratch_shapes=[pltpu.VMEM((ns, t, LANES), BF16),
                        pltpu.VMEM((ns, t // bk, LANES, bk), BF16),
                        pltpu.VMEM((ns, LANES, 2 * gq), BF16),
                        pltpu.VMEM((ns, 1, 2 * gq), F32),
                        pltpu.VMEM((ns, LANES, gq), F32),
                        pltpu.VMEM((bk, bk), F32),
                        pltpu.VMEM((ns, SB_UNROLL, bk, 2 * gq), BF16),
                        pltpu.VMEM((ns, SB_UNROLL, bk, 2 * gq), F32),
                        pltpu.VMEM((ns, SB_UNROLL, 8, 2 * gq), F32),
                        pltpu.VMEM((ns, SB_UNROLL, bk, 2 * gq), BF16)],
        compiler_params=pltpu.CompilerParams(
            dimension_semantics=("arbitrary", "arbitrary", "arbitrary"),
            vmem_limit_bytes=VMEM_LIMIT),
        name="sb_attention",
    )(proj3, proj3, proj3, pair(qg), pair(kg), pair(og), ut)
    return out.reshape(m, n_pairs * LANES)


def kernel(x, norm_mix_g, w_in, gla_w_decay, gla_b_decay, gla_out_g, sb_q_g, sb_k_g, sb_out_g,
           hg_out_g, hg_lb_logits, w_out, norm_ffn_g, w_ffn_up, w_ffn_down):
    batch, seq, d_model = x.shape
    depth = w_in.shape[0]
    x2 = x.reshape(batch * seq, d_model).astype(F32)

    gla_cols = _padded_head_cols(0, GLA_HEADS, GLA_DK, GLA_DK_PAD)
    sb_lo = GLA_HEADS * GLA_DV
    hg_lo = sb_lo + SB_HEADS * SB_DH
    w_main = _take_padded(w_in, _MAIN_SRC, 2).astype(BF16)
    w_gate = _take_padded(w_in, _GATE_SRC, 2).astype(BF16)
    wd = _take_padded(gla_w_decay.astype(F32), gla_cols, 2)
    wd = jnp.pad(wd, ((0, 0), (0, LANES - GLA_LOWRANK), (0, 0)))
    bd = _take_padded(gla_b_decay.astype(F32), gla_cols, 1)[:, None, :]
    gain_a = jnp.pad(gla_out_g.astype(F32), ((0, 0), (0, HEAD_V_PAD - GLA_DV)))[:, None, :]
    gain_c = jnp.tile(hg_out_g.astype(F32), (1, LANES // HG_DV))[:, None, :]
    w_a = _take_padded(w_out, _padded_head_cols(0, GLA_HEADS, GLA_DV, HEAD_V_PAD), 1).astype(BF16)
    w_b = w_out[:, sb_lo:hg_lo].astype(BF16)
    w_c = w_out[:, hg_lo:].astype(BF16)
    w_up = w_ffn_up.astype(BF16)
    w_down = w_ffn_down.astype(BF16)
    lb_logits = hg_lb_logits.astype(F32)
    g_mix = norm_mix_g.astype(F32)[:, None, :]
    g_ffn = norm_ffn_g.astype(F32)[:, None, :]

    for li in range(depth):
        projs = _in_proj(li, x2, g_mix, w_main, w_gate, tm=ROW_TILE, tn=MAIN_COLS // 2)
        o_a, o_c = _mixers(li, projs, wd, bd, gain_a, lb_logits, gain_c, batch=batch)
        o_b = _sb_attention(projs[0], sb_q_g[li].astype(F32), sb_k_g[li].astype(F32),
                            sb_out_g[li].astype(F32), batch=batch)
        x2 = _residual_matmul(li, x2, (o_a, o_b, o_c), (w_a, w_b, w_c), tm=ROW_TILE)
        act = _ffn_up(li, x2, g_ffn, w_up, tm=ROW_TILE, tn=w_up.shape[2] // 4)
        x2 = _residual_matmul(li, x2, (act,), (w_down,), tm=ROW_TILE // 2)
    return x2.reshape(batch, seq, d_model).astype(x.dtype)
```
